```python
import jax, jax.numpy as jnp
from jax import lax
import numpy as np

D_MODEL = 1024
BATCH = 16
SEQ = 4096
DEPTH = 2
DEC_BATCH = 2
DEC_SEQ = 16384
PAST_LEN = 128

GRID_W = 64
N_MIXERS = 2
EPS = 1e-6
F_GROUPS = 4
F_GROUP_DIM = D_MODEL // F_GROUPS
N_HEADS = 16
HEAD_DIM = D_MODEL // N_HEADS
NA_KH = 8
NA_KW = 16
NA_QB = 16
NA_KB = 32
N_EXPERTS = 16
EC_CAPACITY = 2
EXPERT_FF = 2816
NEG_INF = -1e30

kernel_name = "hybrid_fnet_natten_ec_encoder"


def rms_norm(x, g):
    xf = x.astype(jnp.float32)
    r = lax.rsqrt(jnp.mean(xf * xf, axis=-1, keepdims=True) + EPS)
    return (xf * r).astype(x.dtype) * g


def fourier_mixer(h, w_in, w_out):
    B, S, _ = h.shape
    u = (h @ w_in).reshape(B, S, F_GROUPS, F_GROUP_DIM).astype(jnp.float32)
    z = jnp.fft.fft2(u, axes=(1, 3), norm="ortho").real
    return z.astype(h.dtype).reshape(B, S, D_MODEL) @ w_out


def neighbourhood_attention(h, w_qkv, w_o, rpb):
    B, S, _ = h.shape
    rows = S // GRID_W
    kh = min(NA_KH, rows)
    n_cb = GRID_W // NA_QB
    qkv = (h @ w_qkv).reshape(B, rows, GRID_W, 3, N_HEADS, HEAD_DIM)
    q = (qkv[:, :, :, 0] * (HEAD_DIM ** -0.5)).reshape(B, rows, n_cb, NA_QB, N_HEADS, HEAD_DIM)
    k = qkv[:, :, :, 1]
    v = qkv[:, :, :, 2]
    qcol = np.arange(GRID_W).reshape(n_cb, NA_QB)
    kstart = np.clip(NA_QB * np.arange(n_cb) - NA_KW // 2, 0, GRID_W - NA_KB)
    kcol = kstart[:, None] + np.arange(NA_KB)
    wstart = np.clip(qcol - NA_KW // 2, 0, GRID_W - NA_KW)
    kc = kcol[:, None, :]
    mask = (kc >= wstart[:, :, None]) & (kc < wstart[:, :, None] + NA_KW)
    col_bias_idx = np.clip(kc - qcol[:, :, None] + NA_KW - 1, 0, 2 * NA_KW - 2)

    def row_block(r):
        start = jnp.clip(r - kh // 2, 0, rows - kh)
        qr = lax.dynamic_index_in_dim(q, r, axis=1, keepdims=False)
        kr = lax.dynamic_slice_in_dim(k, start, kh, axis=1)[:, :, kcol]
        vr = lax.dynamic_slice_in_dim(v, start, kh, axis=1)[:, :, kcol]
        s = jnp.einsum('bjqhd,bkjchd->bhjqkc', qr, kr).astype(jnp.float32)
        row_bias_idx = start + jnp.arange(kh) - r + NA_KH - 1
        bias = rpb[:, row_bias_idx][:, :, col_bias_idx]
        s = s + jnp.transpose(bias, (0, 2, 3, 1, 4)).astype(jnp.float32)
        s = jnp.where(mask[:, :, None, :], s, NEG_INF)
        p = jax.nn.softmax(s.reshape(B, N_HEADS, n_cb, NA_QB, kh * NA_KB), axis=-1)
        p = p.reshape(B, N_HEADS, n_cb, NA_QB, kh, NA_KB).astype(vr.dtype)
        o = jnp.einsum('bhjqkc,bkjchd->bjqhd', p, vr)
        return o.reshape(B, GRID_W, N_HEADS * HEAD_DIM)

    out = lax.map(row_block, jnp.arange(rows))
    out = jnp.moveaxis(out, 0, 1).reshape(B, S, N_HEADS * HEAD_DIM)
    return out @ w_o


def expert_choice_moe(h, w_router, w_gate, w_up, w_down):
    B, S, _ = h.shape
    xf = h.reshape(B * S, D_MODEL)
    n_tok = B * S
    cap = EC_CAPACITY * n_tok // N_EXPERTS
    probs = jax.nn.softmax((xf @ w_router).astype(jnp.float32), axis=-1)
    gates, idx = lax.top_k(probs.T, cap)

    def expert(args):
        idx_e, g_e, wg, wu, wd = args
        xe = xf[idx_e]
        hid = jax.nn.silu(xe @ wg) * (xe @ wu)
        return (hid @ wd) * g_e[:, None].astype(xe.dtype)

    out_e = lax.map(expert, (idx, gates, w_gate, w_up, w_down))
    y = jnp.zeros_like(xf).at[idx.reshape(-1)].add(out_e.reshape(-1, D_MODEL))
    return y.reshape(B, S, D_MODEL)


def trunk(x, norm_mix, norm_ffn, norm_final, w_fourier_in, w_fourier_out,
          w_qkv, w_attn_out, rel_pos_bias, w_router, w_gate, w_up, w_down):
    for i in range(DEPTH):
        h = rms_norm(x, norm_mix[i])
        j = i // N_MIXERS
        if i % N_MIXERS == 0:
            x = x + fourier_mixer(h, w_fourier_in[j], w_fourier_out[j])
        else:
            x = x + neighbourhood_attention(h, w_qkv[j], w_attn_out[j], rel_pos_bias[j])
        x = x + expert_choice_moe(rms_norm(x, norm_ffn[i]), w_router[i], w_gate[i], w_up[i], w_down[i])
    return rms_norm(x, norm_final)


def setup_inputs(seed: int = 0) -> dict:
    key = jax.random.key(seed)
    ks = jax.random.split(key, 16)
    n_f = (DEPTH + 1) // 2
    n_na = DEPTH // 2
    d = D_MODEL
    nrm = jax.random.normal
    return {
        "x_prompt": nrm(ks[0], (BATCH, SEQ, d), jnp.float32),
        "x_sample": nrm(ks[1], (DEC_BATCH, DEC_SEQ, d), jnp.float32),
        "norm_mix": 1.0 + 0.02 * nrm(ks[2], (DEPTH, d), jnp.float32),
        "norm_ffn": 1.0 + 0.02 * nrm(ks[3], (DEPTH, d), jnp.float32),
        "norm_final": 1.0 + 0.02 * nrm(ks[4], (d,), jnp.float32),
        "w_fourier_in": nrm(ks[5], (n_f, d, d), jnp.float32) * d ** -0.5,
        "w_fourier_out": nrm(ks[6], (n_f, d, d), jnp.float32) * d ** -0.5,
        "w_qkv": nrm(ks[7], (n_na, d, 3 * N_HEADS * HEAD_DIM), jnp.float32) * d ** -0.5,
        "w_attn_out": nrm(ks[8], (n_na, N_HEADS * HEAD_DIM, d), jnp.float32) * (N_HEADS * HEAD_DIM) ** -0.5,
        "rel_pos_bias": 0.02 * nrm(ks[9], (n_na, N_HEADS, 2 * NA_KH - 1, 2 * NA_KW - 1), jnp.float32),
        "w_router": nrm(ks[10], (DEPTH, d, N_EXPERTS), jnp.float32) * d ** -0.5,
        "w_gate": nrm(ks[11], (DEPTH, N_EXPERTS, d, EXPERT_FF), jnp.float32) * d ** -0.5,
        "w_up": nrm(ks[12], (DEPTH, N_EXPERTS, d, EXPERT_FF), jnp.float32) * d ** -0.5,
        "w_down": nrm(ks[13], (DEPTH, N_EXPERTS, EXPERT_FF, d), jnp.float32) * EXPERT_FF ** -0.5,
    }


def reference(x_prompt, x_sample, norm_mix, norm_ffn, norm_final, w_fourier_in,
              w_fourier_out, w_qkv, w_attn_out, rel_pos_bias, w_router, w_gate,
              w_up, w_down):
    y_prompt = trunk(x_prompt, norm_mix, norm_ffn, norm_final, w_fourier_in, w_fourier_out,
                     w_qkv, w_attn_out, rel_pos_bias, w_router, w_gate, w_up, w_down)
    y_sample = trunk(x_sample, norm_mix, norm_ffn, norm_final, w_fourier_in, w_fourier_out,
                     w_qkv, w_attn_out, rel_pos_bias, w_router, w_gate, w_up, w_down)
    return (y_prompt, y_sample)
```

```python
import functools
import math

import numpy as np
import jax
import jax.numpy as jnp
from jax import lax
from jax.experimental import pallas as pl
from jax.experimental.pallas import tpu as pltpu

EPS = 1e-6
GRID_W = 64
N_HEADS = 16
HEAD_DIM = 64
NA_KH = 8
NA_KW = 16
F_GROUPS = 4
EC_CAPACITY = 2
NEG_INF = -1e30

ROW_TILE = 512
ATT_ROWS = 4
MOE_T = 512
MOE_W = 128
MOE_G = 16
FFN_TM = 512
VMEM_LIMIT = 56 * 1024 * 1024

F32 = jnp.float32
BF16 = jnp.bfloat16
I32 = jnp.int32


def _cparams(sem):
    return pltpu.CompilerParams(dimension_semantics=sem, vmem_limit_bytes=VMEM_LIMIT)


def _rms(x, g):
    r = lax.rsqrt(jnp.mean(x * x, axis=-1, keepdims=True) + EPS)
    return (x * r) * g


def _norm_matmul_kernel(x_ref, g_ref, w_ref, o_ref, *, col_chunk):
    h = _rms(x_ref[...], g_ref[...]).astype(BF16)
    for c in range(o_ref.shape[1] // col_chunk):
        sl = slice(c * col_chunk, (c + 1) * col_chunk)
        o_ref[:, sl] = jnp.dot(h, w_ref[:, sl], preferred_element_type=F32).astype(o_ref.dtype)


def norm_matmul(x, g, w):
    n, d = x.shape
    m = w.shape[1]
    return pl.pallas_call(
        functools.partial(_norm_matmul_kernel, col_chunk=min(m, 1024)),
        grid=(n // ROW_TILE,),
        in_specs=[pl.BlockSpec((ROW_TILE, d), lambda i: (i, 0)),
                  pl.BlockSpec((1, d), lambda i: (0, 0)),
                  pl.BlockSpec((d, m), lambda i: (0, 0))],
        out_specs=pl.BlockSpec((ROW_TILE, m), lambda i: (i, 0)),
        out_shape=jax.ShapeDtypeStruct((n, m), BF16),
        compiler_params=_cparams(("parallel",)),
    )(x, g.reshape(1, d), w)


def _fourier_in_kernel(x_ref, g_ref, w_ref, cs_ref, vr_ref, vi_ref):
    h = _rms(x_ref[...], g_ref[...]).astype(BF16)
    u = jnp.dot(h, w_ref[...], preferred_element_type=F32).astype(BF16)
    gd = cs_ref.shape[0]
    for gi in range(u.shape[1] // gd):
        sl = slice(gi * gd, (gi + 1) * gd)
        ab = jnp.dot(u[:, sl], cs_ref[...], preferred_element_type=F32)
        vr_ref[:, sl] = ab[:, :gd].astype(BF16)
        vi_ref[:, sl] = ab[:, gd:].astype(BF16)


def _dft_a_kernel(vr_ref, vi_ref, m_ref, yr_ref, yi_ref):
    n1 = vr_ref.shape[1]
    v = jnp.concatenate([vr_ref[0], vi_ref[0]], axis=0)
    y = jnp.dot(m_ref[...], v, preferred_element_type=F32)
    yr_ref[0] = y[:n1].astype(BF16)
    yi_ref[0] = y[n1:].astype(BF16)


def _dft_c_kernel(yr_ref, yi_ref, g_ref, z_ref):
    kb, c = yr_ref.shape[1], yr_ref.shape[3]
    for i in range(kb):
        ycat = jnp.concatenate([yr_ref[0, i], yi_ref[0, i]], axis=0)
        z = jnp.dot(g_ref[i], ycat, preferred_element_type=F32)
        z_ref[0, :, i * c:(i + 1) * c] = z.astype(BF16)


def _dft_tables(s, n1, n2, gd):
    def cs(num, den):
        ang = (2.0 * math.pi / den) * (num % den).astype(F32)
        return jnp.cos(ang), jnp.sin(ang)

    ic = jnp.arange(gd, dtype=I32)
    cc, sc = cs(ic[:, None] * ic[None, :], gd)
    chan = jnp.concatenate([cc, -sc], axis=1).astype(BF16)
    i1 = jnp.arange(n1, dtype=I32)
    c1, s1 = cs(i1[:, None] * i1[None, :], n1)
    m1 = jnp.concatenate([jnp.concatenate([c1, s1], axis=1),
                          jnp.concatenate([-s1, c1], axis=1)], axis=0).astype(BF16)
    k1 = jnp.arange(n1, dtype=I32)[:, None, None]
    k2 = jnp.arange(n2, dtype=I32)[None, :, None]
    s2 = jnp.arange(n2, dtype=I32)[None, None, :]
    cg, sg = cs(((n1 * k2 + k1) % s) * s2, s)
    scale = 1.0 / math.sqrt(float(s) * float(gd))
    g = (jnp.concatenate([cg, sg], axis=2) * scale).astype(BF16)
    return chan, m1, g


def fourier_core(x, g_norm, w_in, batch, seq):
    n, d = x.shape
    gd = d // F_GROUPS
    n1 = 1 << ((seq.bit_length() - 1 + 1) // 2)
    n2 = seq // n1
    assert n1 * n2 == seq and n1 % 16 == 0 and n2 % 16 == 0
    chan, m1, gtab = _dft_tables(seq, n1, n2, gd)

    vr, vi = pl.pallas_call(
        _fourier_in_kernel,
        grid=(n // ROW_TILE,),
        in_specs=[pl.BlockSpec((ROW_TILE, d), lambda i: (i, 0)),
                  pl.BlockSpec((1, d), lambda i: (0, 0)),
                  pl.BlockSpec((d, d), lambda i: (0, 0)),
                  pl.BlockSpec((gd, 2 * gd), lambda i: (0, 0))],
        out_specs=[pl.BlockSpec((ROW_TILE, d), lambda i: (i, 0))] * 2,
        out_shape=[jax.ShapeDtypeStruct((n, d), BF16)] * 2,
        compiler_params=_cparams(("parallel",)),
    )(x, g_norm.reshape(1, d), w_in, chan)

    cols = n2 * d
    nc = min(cols, 4096)
    vr = vr.reshape(batch, n1, cols)
    vi = vi.reshape(batch, n1, cols)
    yr, yi = pl.pallas_call(
        _dft_a_kernel,
        grid=(batch, cols // nc),
        in_specs=[pl.BlockSpec((1, n1, nc), lambda b, j: (b, 0, j)),
                  pl.BlockSpec((1, n1, nc), lambda b, j: (b, 0, j)),
                  pl.BlockSpec((2 * n1, 2 * n1), lambda b, j: (0, 0))],
        out_specs=[pl.BlockSpec((1, n1, nc), lambda b, j: (b, 0, j))] * 2,
        out_shape=[jax.ShapeDtypeStruct((batch, n1, cols), BF16)] * 2,
        compiler_params=_cparams(("parallel", "parallel")),
    )(vr, vi, m1)

    kb = 8
    yr = yr.reshape(batch, n1, n2, d)
    yi = yi.reshape(batch, n1, n2, d)
    z = pl.pallas_call(
        _dft_c_kernel,
        grid=(batch, n1 // kb),
        in_specs=[pl.BlockSpec((1, kb, n2, d), lambda b, k: (b, k, 0, 0)),
                  pl.BlockSpec((1, kb, n2, d), lambda b, k: (b, k, 0, 0)),
                  pl.BlockSpec((kb, n2, 2 * n2), lambda b, k: (k, 0, 0))],
        out_specs=pl.BlockSpec((1, n2, kb * d), lambda b, k: (b, 0, k)),
        out_shape=jax.ShapeDtypeStruct((batch, n2, n1 * d), BF16),
        compiler_params=_cparams(("parallel", "parallel")),
    )(yr, yi, gtab)
    return z.reshape(n, d)


def _attn_kernel(q_ref, kp_ref, kc_ref, kn_ref, vp_ref, vc_ref, vn_ref, bias_ref, mask_ref, o_ref):
    q = q_ref[...] * jnp.asarray(HEAD_DIM ** -0.5, BF16)
    k = jnp.concatenate([kp_ref[...], kc_ref[...], kn_ref[...]], axis=0)
    v = jnp.concatenate([vp_ref[...], vc_ref[...], vn_ref[...]], axis=0)
    mask = mask_ref[0]
    lane = lax.broadcasted_iota(I32, (1, 2 * HEAD_DIM), 1)
    for hp in range(N_HEADS // 2):
        sl = slice(hp * 2 * HEAD_DIM, (hp + 1) * 2 * HEAD_DIM)
        q2, k2, v2 = q[:, sl], k[:, sl], v[:, sl]
        out = None
        for s in range(2):
            in_head = (lane >= s * HEAD_DIM) & (lane < (s + 1) * HEAD_DIM)
            qm = jnp.where(in_head, q2, jnp.zeros_like(q2))
            sc = lax.dot_general(qm, k2, (((1,), (1,)), ((), ())), preferred_element_type=F32)
            sc = sc + bias_ref[2 * hp + s] + mask
            m = jnp.max(sc, axis=1, keepdims=True)
            p = jnp.exp(sc - m)
            l = jnp.sum(p, axis=1, keepdims=True)
            pv = jnp.dot(p.astype(BF16), v2, preferred_element_type=F32) / l
            pv = jnp.where(in_head, pv, 0.0)
            out = pv if out is None else out + pv
        o_ref[:, sl] = out.astype(BF16)


def _attn_tables(rpb):
    iq = np.arange(ATT_ROWS)[:, None, None, None]
    c = np.arange(GRID_W)[None, :, None, None]
    kr = np.arange(3 * ATT_ROWS)[None, None, :, None]
    kc = np.arange(GRID_W)[None, None, None, :]
    shape = (ATT_ROWS, GRID_W, 3 * ATT_ROWS, GRID_W)
    dr = np.broadcast_to(kr - iq + (NA_KH - 1 - ATT_ROWS), shape).reshape(ATT_ROWS * GRID_W, -1)
    dc = np.broadcast_to(np.clip(kc - c + NA_KW - 1, 0, 2 * NA_KW - 2), shape).reshape(ATT_ROWS * GRID_W, -1)
    bias = rpb[:, dr, dc].astype(F32)
    ws = np.clip(c - NA_KW // 2, 0, GRID_W - NA_KW)
    col_ok = (kc >= ws) & (kc < ws + NA_KW)
    row_ok = [
        (kr >= ATT_ROWS) & (kr < ATT_ROWS + NA_KH) & (iq >= 0),
        (kr >= iq) & (kr < iq + NA_KH),
        (kr < NA_KH) & (iq >= 0),
    ]
    mask = np.stack([np.where(np.broadcast_to(r & col_ok, shape), 0.0, NEG_INF).reshape(ATT_ROWS * GRID_W, -1)
                     for r in row_ok]).astype(np.float32)
    return bias, jnp.asarray(mask)


def attention_core(qkv, rpb, batch, seq):
    n = qkv.shape[0]
    d = N_HEADS * HEAD_DIM
    rows = seq // GRID_W
    groups = rows // ATT_ROWS
    assert rows % ATT_ROWS == 0 and rows >= 4 * ATT_ROWS and NA_KH == 2 * ATT_ROWS
    rq = ATT_ROWS * GRID_W
    bias, mask = _attn_tables(rpb)

    def blk(off, col):
        def index_map(b, g):
            return (b * groups + jnp.clip(g + off, 0, groups - 1), col)
        return pl.BlockSpec((rq, d), index_map)

    def case_map(b, g):
        return (jnp.where(g == 0, 0, jnp.where(g == groups - 1, 2, 1)), 0, 0)

    return pl.pallas_call(
        _attn_kernel,
        grid=(batch, groups),
        in_specs=[blk(0, 0), blk(-1, 1), blk(0, 1), blk(1, 1), blk(-1, 2), blk(0, 2), blk(1, 2),
                  pl.BlockSpec((N_HEADS, rq, 3 * rq), lambda b, g: (0, 0, 0)),
                  pl.BlockSpec((1, rq, 3 * rq), case_map)],
        out_specs=pl.BlockSpec((rq, d), lambda b, g: (b * groups + g, 0)),
        out_shape=jax.ShapeDtypeStruct((n, d), BF16),
        compiler_params=_cparams(("parallel", "parallel")),
    )(qkv, qkv, qkv, qkv, qkv, qkv, qkv, bias, mask)


def _post_kernel(a_ref, x_ref, wo_ref, g_ref, wr_ref, xo_ref, h_ref, p_ref):
    xn = x_ref[...] + jnp.dot(a_ref[...], wo_ref[...], preferred_element_type=F32)
    xo_ref[...] = xn
    hf = _rms(xn, g_ref[...])
    h_ref[...] = hf.astype(BF16)
    logits = lax.dot_general(wr_ref[...], hf, (((1,), (1,)), ((), ())),
                             precision=lax.Precision.HIGHEST, preferred_element_type=F32)
    m = jnp.max(logits, axis=0, keepdims=True)
    e = jnp.exp(logits - m)
    p_ref[...] = e / jnp.sum(e, axis=0, keepdims=True)


def post_mixer(a, x, w_o, g_ffn, w_router_t):
    n, d = x.shape
    e = w_router_t.shape[0]
    return pl.pallas_call(
        _post_kernel,
        grid=(n // ROW_TILE,),
        in_specs=[pl.BlockSpec((ROW_TILE, d), lambda i: (i, 0)),
                  pl.BlockSpec((ROW_TILE, d), lambda i: (i, 0)),
                  pl.BlockSpec((d, d), lambda i: (0, 0)),
                  pl.BlockSpec((1, d), lambda i: (0, 0)),
                  pl.BlockSpec((e, d), lambda i: (0, 0))],
        out_specs=[pl.BlockSpec((ROW_TILE, d), lambda i: (i, 0)),
                   pl.BlockSpec((ROW_TILE, d), lambda i: (i, 0)),
                   pl.BlockSpec((e, ROW_TILE), lambda i: (0, i))],
        out_shape=[jax.ShapeDtypeStruct((n, d), F32),
                   jax.ShapeDtypeStruct((n, d), BF16),
                   jax.ShapeDtypeStruct((e, n), F32)],
        compiler_params=_cparams(("parallel",)),
    )(a, x, w_o, g_ffn.reshape(1, d), w_router_t)


def _threshold_kernel(p_ref, thr_ref, need_ref, *, cap):
    e = p_ref.shape[0]

    def count_ge(cand):
        bits = pltpu.bitcast(p_ref[...], I32)
        return jnp.sum((bits >= cand).astype(F32), axis=1, keepdims=True)

    def body(i, prefix):
        cand = prefix | jnp.left_shift(jnp.int32(1), 30 - i)
        return jnp.where(count_ge(cand) >= cap, cand, prefix)

    thr = lax.fori_loop(0, 31, body, jnp.zeros((e, 1), I32))
    n_gt = count_ge(thr + 1)
    thr_ref[...] = jnp.broadcast_to(thr, thr_ref.shape)
    need_ref[...] = jnp.broadcast_to(cap - n_gt, need_ref.shape)


def _select_kernel(p_ref, thr_ref, need_ref, ut_ref, gate_ref, pos_ref, cnt_ref, carry_ref):
    @pl.when(pl.program_id(0) == 0)
    def _():
        carry_ref[...] = jnp.zeros_like(carry_ref)

    p = p_ref[...]
    bits = pltpu.bitcast(p, I32)
    thr = thr_ref[:, :1]
    gt = bits > thr
    eq = bits == thr
    ut = ut_ref[...]
    eq_f = jnp.where(eq, 1.0, 0.0)
    eq_before = jnp.dot(eq_f.astype(BF16), ut, preferred_element_type=F32) + carry_ref[:, :1]
    sel = gt | (eq & (eq_before < need_ref[:, :1]))
    carry_ref[...] = carry_ref[...] + jnp.sum(eq_f, axis=1, keepdims=True)
    sel_f = jnp.where(sel, 1.0, 0.0)
    rank = jnp.dot(sel_f.astype(BF16), ut, preferred_element_type=F32)
    pos_ref[...] = jnp.where(sel, rank.astype(I32), -1)
    gate_ref[...] = jnp.where(sel, p, 0.0)
    cnt = jnp.sum(sel_f, axis=1, keepdims=True).astype(I32)
    cnt_ref[0] = jnp.broadcast_to(cnt, cnt_ref.shape[1:])


def expert_choice_select(probs_t):
    e, n = probs_t.shape
    cap = EC_CAPACITY * n // e
    j = n // MOE_T
    thr, need = pl.pallas_call(
        functools.partial(_threshold_kernel, cap=cap),
        out_shape=[jax.ShapeDtypeStruct((e, 128), I32), jax.ShapeDtypeStruct((e, 128), F32)],
        compiler_params=pltpu.CompilerParams(vmem_limit_bytes=VMEM_LIMIT),
    )(probs_t)
    it = np.arange(MOE_T)
    ut = jnp.asarray((it[:, None] < it[None, :]).astype(np.float32), BF16)
    gates, pos, cnt = pl.pallas_call(
        _select_kernel,
        grid=(j,),
        in_specs=[pl.BlockSpec((e, MOE_T), lambda i: (0, i)),
                  pl.BlockSpec((e, 128), lambda i: (0, 0)),
                  pl.BlockSpec((e, 128), lambda i: (0, 0)),
                  pl.BlockSpec((MOE_T, MOE_T), lambda i: (0, 0))],
        out_specs=[pl.BlockSpec((e, MOE_T), lambda i: (0, i)),
                   pl.BlockSpec((e, MOE_T), lambda i: (0, i)),
                   pl.BlockSpec((1, e, 128), lambda i: (i, 0, 0))],
        out_shape=[jax.ShapeDtypeStruct((e, n), F32),
                   jax.ShapeDtypeStruct((e, n), I32),
                   jax.ShapeDtypeStruct((j, e, 128), I32)],
        scratch_shapes=[pltpu.VMEM((e, 128), F32)],
        compiler_params=_cparams(("arbitrary",)),
    )(probs_t, thr, need, ut)
    return gates, pos, cnt[:, :, 0].T


def _one_hot_slots(pos_ref, base):
    e, t = pos_ref.shape
    rows = lax.broadcasted_iota(I32, (MOE_W, t), 0) + base
    blocks = [jnp.where(rows == pos_ref[i:i + 1, :], 1.0, 0.0).astype(BF16) for i in range(e)]
    return jnp.concatenate(blocks, axis=0)


def _gather_kernel(a_ref, npass_ref, len_ref, h_ref, gate_ref, pos_ref, xs_ref, gs_ref,
                   ybuf, gbuf, zx, zg, sem, *, region, zero_rows, zero_waves):
    j = pl.program_id(0)
    nj = pl.num_programs(0)
    e, t = pos_ref.shape
    g = gate_ref[...]
    g_hi = g.astype(BF16)
    r1 = g - g_hi.astype(F32)
    g_mid = r1.astype(BF16)
    g_lo = (r1 - g_mid.astype(F32)).astype(BF16)
    g3 = jnp.concatenate([g_hi, g_mid, g_lo], axis=0)
    lane = lax.broadcasted_iota(I32, (1, 3 * e), 1)

    def copies(base):
        out = []
        for i in range(e):
            dst = pl.multiple_of(i * region + a_ref[i * nj + j] + base, MOE_G)
            src = pl.ds(i * MOE_W, MOE_W)
            out.append(pltpu.make_async_copy(ybuf.at[src], xs_ref.at[pl.ds(dst, MOE_W)], sem.at[0]))
            out.append(pltpu.make_async_copy(gbuf.at[src], gs_ref.at[pl.ds(dst, MOE_W)], sem.at[1]))
        return out

    def one_pass(p, carry):
        base = p * MOE_W
        onehot = _one_hot_slots(pos_ref, base)
        ybuf[...] = jnp.dot(onehot, h_ref[...], preferred_element_type=F32).astype(BF16)
        gs = lax.dot_general(onehot, g3, (((1,), (1,)), ((), ())), preferred_element_type=F32)
        for i in range(e):
            own = (lane % e) == i
            gi = jnp.sum(jnp.where(own, gs[i * MOE_W:(i + 1) * MOE_W], 0.0), axis=1, keepdims=True)
            gbuf[i * MOE_W:(i + 1) * MOE_W, :] = jnp.broadcast_to(gi, (MOE_W, gbuf.shape[1]))
        cps = copies(base)
        for cp in cps:
            cp.start()
        for cp in cps:
            cp.wait()
        return carry

    lax.fori_loop(0, npass_ref[j], one_pass, 0)

    @pl.when(j == nj - 1)
    def _():
        zx[...] = jnp.zeros_like(zx)
        zg[...] = jnp.zeros_like(zg)
        for k in range(zero_waves):
            cps = []
            for i in range(e):
                start = jnp.minimum(len_ref[i] + k * zero_rows, region - zero_rows)
                dst = pl.multiple_of(i * region + start, MOE_G)
                cps.append(pltpu.make_async_copy(zx, xs_ref.at[pl.ds(dst, zero_rows)], sem.at[0]))
                cps.append(pltpu.make_async_copy(zg, gs_ref.at[pl.ds(dst, zero_rows)], sem.at[1]))
            for cp in cps:
                cp.start()
            for cp in cps:
                cp.wait()


def _ffn_kernel(nt_ref, x_ref, gs_ref, wg_ref, wu_ref, wd_ref, o_ref, acc_ref):
    ei, m, f = pl.program_id(0), pl.program_id(1), pl.program_id(2)

    @pl.when(m < nt_ref[ei])
    def _():
        @pl.when(f == 0)
        def _():
            acc_ref[...] = jnp.zeros_like(acc_ref)

        x = x_ref[...]
        gate = jnp.dot(x, wg_ref[0], preferred_element_type=F32)
        up = jnp.dot(x, wu_ref[0], preferred_element_type=F32)
        hid = (gate * jax.nn.sigmoid(gate) * up).astype(BF16)
        acc_ref[...] += jnp.dot(hid, wd_ref[0], preferred_element_type=F32)

        @pl.when(f == pl.num_programs(2) - 1)
        def _():
            o_ref[...] = (acc_ref[...] * gs_ref[:, :1]).astype(BF16)

    @pl.when((m >= nt_ref[ei]) & (f == pl.num_programs(2) - 1))
    def _():
        o_ref[...] = jnp.zeros_like(o_ref)


def _combine_kernel(a_ref, npass_ref, len_ref, x_ref, pos_ref, gfin_ref, o_hbm, out_ref, obuf, sem,
                    *, region, final_norm):
    j = pl.program_id(0)
    nj = pl.num_programs(0)
    e = pos_ref.shape[0]

    def one_pass(p, acc):
        base = p * MOE_W
        cps = []
        for i in range(e):
            start = jnp.minimum(a_ref[i * nj + j] + base, len_ref[i])
            src = pl.multiple_of(i * region + start, MOE_G)
            cps.append(pltpu.make_async_copy(o_hbm.at[pl.ds(src, MOE_W)], obuf.at[pl.ds(i * MOE_W, MOE_W)], sem.at[0]))
        for cp in cps:
            cp.start()
        onehot = _one_hot_slots(pos_ref, base)
        for cp in cps:
            cp.wait()
        return acc + lax.dot_general(onehot, obuf[...], (((0,), (0,)), ((), ())), preferred_element_type=F32)

    y = lax.fori_loop(0, npass_ref[j], one_pass, x_ref[...])
    out_ref[...] = _rms(y, gfin_ref[...]) if final_norm else y


def expert_choice_moe(x, h, probs_t, w_gate, w_up, w_down, g_final=None):
    n, d = x.shape
    e = probs_t.shape[0]
    f = w_gate.shape[2]
    cap = EC_CAPACITY * n // e
    nj = n // MOE_T
    gates, pos, cnt = expert_choice_select(probs_t)

    padded = (cnt + MOE_G - 1) // MOE_G * MOE_G
    ends = jnp.cumsum(padded, axis=1)
    starts = (ends - padded).astype(I32).reshape(-1)
    lens = ends[:, -1].astype(I32)
    npass = jnp.maximum(1, (jnp.max(cnt, axis=0) + MOE_W - 1) // MOE_W).astype(I32)
    zero_rows = FFN_TM + MOE_W
    region = -(-(cap + MOE_G * nj + MOE_T + zero_rows) // FFN_TM) * FFN_TM
    ntiles = ((lens + MOE_W + FFN_TM - 1) // FFN_TM).astype(I32)
    mt = region // FFN_TM

    xs, gs = pl.pallas_call(
        functools.partial(_gather_kernel, region=region, zero_rows=zero_rows,
                          zero_waves=-(-(region - cap) // zero_rows)),
        grid_spec=pltpu.PrefetchScalarGridSpec(
            num_scalar_prefetch=3,
            grid=(nj,),
            in_specs=[pl.BlockSpec((MOE_T, d), lambda i, *_: (i, 0)),
                      pl.BlockSpec((e, MOE_T), lambda i, *_: (0, i)),
                      pl.BlockSpec((e, MOE_T), lambda i, *_: (0, i))],
            out_specs=[pl.BlockSpec(memory_space=pl.ANY), pl.BlockSpec(memory_space=pl.ANY)],
            scratch_shapes=[pltpu.VMEM((e * MOE_W, d), BF16),
                            pltpu.VMEM((e * MOE_W, 128), F32),
                            pltpu.VMEM((zero_rows, d), BF16),
                            pltpu.VMEM((zero_rows, 128), F32),
                            pltpu.SemaphoreType.DMA((2,))]),
        out_shape=[jax.ShapeDtypeStruct((e * region, d), BF16),
                   jax.ShapeDtypeStruct((e * region, 128), F32)],
        compiler_params=_cparams(("arbitrary",)),
    )(starts, npass, lens, h, gates, pos)

    fc = f // 2 if (f > 1024 and (f // 2) % 128 == 0) else f
    ft = f // fc

    def row_map(ei, m, fi, nt):
        return (ei * mt + jnp.minimum(m, nt[ei] - 1), 0)

    def fchunk(ei, m, fi, nt):
        return jnp.where(m < nt[ei], fi, ft - 1)

    out_slots = pl.pallas_call(
        _ffn_kernel,
        grid_spec=pltpu.PrefetchScalarGridSpec(
            num_scalar_prefetch=1,
            grid=(e, mt, ft),
            in_specs=[pl.BlockSpec((FFN_TM, d), row_map),
                      pl.BlockSpec((FFN_TM, 128), row_map),
                      pl.BlockSpec((1, d, fc), lambda ei, m, fi, nt: (ei, 0, fchunk(ei, m, fi, nt))),
                      pl.BlockSpec((1, d, fc), lambda ei, m, fi, nt: (ei, 0, fchunk(ei, m, fi, nt))),
                      pl.BlockSpec((1, fc, d), lambda ei, m, fi, nt: (ei, fchunk(ei, m, fi, nt), 0))],
            out_specs=pl.BlockSpec((FFN_TM, d), lambda ei, m, fi, nt: (ei * mt + m, 0)),
            scratch_shapes=[pltpu.VMEM((FFN_TM, d), F32)]),
        out_shape=jax.ShapeDtypeStruct((e * region, d), BF16),
        compiler_params=_cparams(("arbitrary", "arbitrary", "arbitrary")),
    )(ntiles, xs, gs, w_gate, w_up, w_down)

    final_norm = g_final is not None
    gfin = (g_final if final_norm else jnp.ones((d,), F32)).reshape(1, d)
    return pl.pallas_call(
        functools.partial(_combine_kernel, region=region, final_norm=final_norm),
        grid_spec=pltpu.PrefetchScalarGridSpec(
            num_scalar_prefetch=3,
            grid=(nj,),
            in_specs=[pl.BlockSpec((MOE_T, d), lambda i, *_: (i, 0)),
                      pl.BlockSpec((e, MOE_T), lambda i, *_: (0, i)),
                      pl.BlockSpec((1, d), lambda i, *_: (0, 0)),
                      pl.BlockSpec(memory_space=pl.ANY)],
            out_specs=pl.BlockSpec((MOE_T, d), lambda i, *_: (i, 0)),
            scratch_shapes=[pltpu.VMEM((e * MOE_W, d), BF16),
                            pltpu.SemaphoreType.DMA((1,))]),
        out_shape=jax.ShapeDtypeStruct((n, d), F32),
        compiler_params=_cparams(("arbitrary",)),
    )(starts, npass, lens, x, pos, gfin, out_slots)


def _trunk(x3, params):
    batch, seq, d = x3.shape
    x = x3.reshape(batch * seq, d)
    depth = params["norm_mix"].shape[0]
    for i in range(depth):
        jm = i // 2
        if i % 2 == 0:
            a = fourier_core(x, params["norm_mix"][i], params["w_fourier_in"][jm], batch, seq)
            w_o = params["w_fourier_out"][jm]
        else:
            qkv = norm_matmul(x, params["norm_mix"][i], params["w_qkv"][jm])
            a = attention_core(qkv, params["rel_pos_bias"][jm], batch, seq)
            w_o = params["w_attn_out"][jm]
        x, h, probs_t = post_mixer(a, x, w_o, params["norm_ffn"][i], params["w_router_t"][i])
        g_final = params["norm_final"] if i == depth - 1 else None
        x = expert_choice_moe(x, h, probs_t, params["w_gate"][i], params["w_up"][i], params["w_down"][i], g_final)
    return x.reshape(batch, seq, d)


def kernel(x_prompt, x_sample, norm_mix, norm_ffn, norm_final, w_fourier_in, w_fourier_out, w_qkv, w_attn_out,
           rel_pos_bias, w_router, w_gate, w_up, w_down):
    params = dict(
        norm_mix=norm_mix, norm_ffn=norm_ffn, norm_final=norm_final, rel_pos_bias=rel_pos_bias,
        w_fourier_in=w_fourier_in.astype(BF16), w_fourier_out=w_fourier_out.astype(BF16),
        w_qkv=w_qkv.astype(BF16), w_attn_out=w_attn_out.astype(BF16),
        w_router_t=jnp.swapaxes(w_router, 1, 2),
        w_gate=w_gate.astype(BF16), w_up=w_up.astype(BF16), w_down=w_down.astype(BF16))
    return (_trunk(x_prompt, params), _trunk(x_sample, params))
```

```python
import functools
import math

import numpy as np
import jax
import jax.numpy as jnp
from jax import lax
from jax.experimental import pallas as pl
from jax.experimental.pallas import tpu as pltpu

EPS = 1e-6
GRID_W = 64
N_HEADS = 16
HEAD_DIM = 64
NA_KH = 8
NA_KW = 16
F_GROUPS = 4
EC_CAPACITY = 2
NEG_INF = -1e30

ROW_TILE = 512
ATT_ROWS = 4
MOE_T = 256
MOE_W = 64
SUB = 16
GATE_COLS = 128
VMEM_LIMIT = 56 * 1024 * 1024

F32 = jnp.float32
BF16 = jnp.bfloat16
I32 = jnp.int32


def _cparams(sem):
    return pltpu.CompilerParams(dimension_semantics=sem, vmem_limit_bytes=VMEM_LIMIT)


def _rms(x, g):
    r = lax.rsqrt(jnp.mean(x * x, axis=-1, keepdims=True) + EPS)
    return (x * r) * g


def _norm_matmul_kernel(x_ref, g_ref, w_ref, o_ref, *, col_chunk):
    h = _rms(x_ref[...], g_ref[...]).astype(BF16)
    for c in range(o_ref.shape[1] // col_chunk):
        sl = slice(c * col_chunk, (c + 1) * col_chunk)
        o_ref[:, sl] = jnp.dot(h, w_ref[:, sl], preferred_element_type=F32).astype(o_ref.dtype)


def norm_matmul(x, g, w):
    n, d = x.shape
    m = w.shape[1]
    return pl.pallas_call(
        functools.partial(_norm_matmul_kernel, col_chunk=min(m, 1024)),
        grid=(n // ROW_TILE,),
        in_specs=[pl.BlockSpec((ROW_TILE, d), lambda i: (i, 0)),
                  pl.BlockSpec((1, d), lambda i: (0, 0)),
                  pl.BlockSpec((d, m), lambda i: (0, 0))],
        out_specs=pl.BlockSpec((ROW_TILE, m), lambda i: (i, 0)),
        out_shape=jax.ShapeDtypeStruct((n, m), BF16),
        compiler_params=_cparams(("parallel",)),
    )(x, g.reshape(1, d), w)


def _fourier_in_kernel(x_ref, g_ref, w_ref, cs_ref, vr_ref, vi_ref):
    h = _rms(x_ref[...], g_ref[...]).astype(BF16)
    u = jnp.dot(h, w_ref[...], preferred_element_type=F32).astype(BF16)
    gd = cs_ref.shape[0]
    for gi in range(u.shape[1] // gd):
        sl = slice(gi * gd, (gi + 1) * gd)
        ab = jnp.dot(u[:, sl], cs_ref[...], preferred_element_type=F32)
        vr_ref[:, sl] = ab[:, :gd].astype(BF16)
        vi_ref[:, sl] = ab[:, gd:].astype(BF16)


def _dft_a_kernel(vr_ref, vi_ref, m_ref, yr_ref, yi_ref):
    n1 = vr_ref.shape[1]
    v = jnp.concatenate([vr_ref[0], vi_ref[0]], axis=0)
    y = jnp.dot(m_ref[...], v, preferred_element_type=F32)
    yr_ref[0] = y[:n1].astype(BF16)
    yi_ref[0] = y[n1:].astype(BF16)


def _dft_c_kernel(yr_ref, yi_ref, g_ref, z_ref):
    kb, c = yr_ref.shape[1], yr_ref.shape[3]
    for i in range(kb):
        ycat = jnp.concatenate([yr_ref[0, i], yi_ref[0, i]], axis=0)
        z = jnp.dot(g_ref[i], ycat, preferred_element_type=F32)
        z_ref[0, :, i * c:(i + 1) * c] = z.astype(BF16)


def _dft_tables(s, n1, n2, gd):
    def cs(num, den):
        ang = (2.0 * math.pi / den) * (num % den).astype(F32)
        return jnp.cos(ang), jnp.sin(ang)

    ic = jnp.arange(gd, dtype=I32)
    cc, sc = cs(ic[:, None] * ic[None, :], gd)
    chan = jnp.concatenate([cc, -sc], axis=1).astype(BF16)
    i1 = jnp.arange(n1, dtype=I32)
    c1, s1 = cs(i1[:, None] * i1[None, :], n1)
    m1 = jnp.concatenate([jnp.concatenate([c1, s1], axis=1),
                          jnp.concatenate([-s1, c1], axis=1)], axis=0).astype(BF16)
    k1 = jnp.arange(n1, dtype=I32)[:, None, None]
    k2 = jnp.arange(n2, dtype=I32)[None, :, None]
    s2 = jnp.arange(n2, dtype=I32)[None, None, :]
    cg, sg = cs(((n1 * k2 + k1) % s) * s2, s)
    scale = 1.0 / math.sqrt(float(s) * float(gd))
    g = (jnp.concatenate([cg, sg], axis=2) * scale).astype(BF16)
    return chan, m1, g


def fourier_core(x, g_norm, w_in, batch, seq):
    n, d = x.shape
    gd = d // F_GROUPS
    n1 = 1 << ((seq.bit_length() - 1 + 1) // 2)
    n2 = seq // n1
    assert n1 * n2 == seq and n1 % 16 == 0 and n2 % 16 == 0
    chan, m1, gtab = _dft_tables(seq, n1, n2, gd)

    vr, vi = pl.pallas_call(
        _fourier_in_kernel,
        grid=(n // ROW_TILE,),
        in_specs=[pl.BlockSpec((ROW_TILE, d), lambda i: (i, 0)),
                  pl.BlockSpec((1, d), lambda i: (0, 0)),
                  pl.BlockSpec((d, d), lambda i: (0, 0)),
                  pl.BlockSpec((gd, 2 * gd), lambda i: (0, 0))],
        out_specs=[pl.BlockSpec((ROW_TILE, d), lambda i: (i, 0))] * 2,
        out_shape=[jax.ShapeDtypeStruct((n, d), BF16)] * 2,
        compiler_params=_cparams(("parallel",)),
    )(x, g_norm.reshape(1, d), w_in, chan)

    cols = n2 * d
    nc = min(cols, 4096)
    vr = vr.reshape(batch, n1, cols)
    vi = vi.reshape(batch, n1, cols)
    yr, yi = pl.pallas_call(
        _dft_a_kernel,
        grid=(batch, cols // nc),
        in_specs=[pl.BlockSpec((1, n1, nc), lambda b, j: (b, 0, j)),
                  pl.BlockSpec((1, n1, nc), lambda b, j: (b, 0, j)),
                  pl.BlockSpec((2 * n1, 2 * n1), lambda b, j: (0, 0))],
        out_specs=[pl.BlockSpec((1, n1, nc), lambda b, j: (b, 0, j))] * 2,
        out_shape=[jax.ShapeDtypeStruct((batch, n1, cols), BF16)] * 2,
        compiler_params=_cparams(("parallel", "parallel")),
    )(vr, vi, m1)

    kb = 8
    yr = yr.reshape(batch, n1, n2, d)
    yi = yi.reshape(batch, n1, n2, d)
    z = pl.pallas_call(
        _dft_c_kernel,
        grid=(batch, n1 // kb),
        in_specs=[pl.BlockSpec((1, kb, n2, d), lambda b, k: (b, k, 0, 0)),
                  pl.BlockSpec((1, kb, n2, d), lambda b, k: (b, k, 0, 0)),
                  pl.BlockSpec((kb, n2, 2 * n2), lambda b, k: (k, 0, 0))],
        out_specs=pl.BlockSpec((1, n2, kb * d), lambda b, k: (b, 0, k)),
        out_shape=jax.ShapeDtypeStruct((batch, n2, n1 * d), BF16),
        compiler_params=_cparams(("parallel", "parallel")),
    )(yr, yi, gtab)
    return z.reshape(n, d)


def _attn_kernel(q_ref, kp_ref, kc_ref, kn_ref, vp_ref, vc_ref, vn_ref, bias_ref, mask_ref, o_ref):
    q = q_ref[...] * jnp.asarray(HEAD_DIM ** -0.5, BF16)
    k = jnp.concatenate([kp_ref[...], kc_ref[...], kn_ref[...]], axis=0)
    v = jnp.concatenate([vp_ref[...], vc_ref[...], vn_ref[...]], axis=0)
    mask = mask_ref[0]
    lane = lax.broadcasted_iota(I32, (1, 2 * HEAD_DIM), 1)
    for hp in range(N_HEADS // 2):
        sl = slice(hp * 2 * HEAD_DIM, (hp + 1) * 2 * HEAD_DIM)
        q2, k2, v2 = q[:, sl], k[:, sl], v[:, sl]
        out = None
        for s in range(2):
            in_head = (lane >= s * HEAD_DIM) & (lane < (s + 1) * HEAD_DIM)
            qm = jnp.where(in_head, q2, jnp.zeros_like(q2))
            sc = lax.dot_general(qm, k2, (((1,), (1,)), ((), ())), preferred_element_type=F32)
            sc = sc + bias_ref[2 * hp + s] + mask
            m = jnp.max(sc, axis=1, keepdims=True)
            p = jnp.exp(sc - m)
            l = jnp.sum(p, axis=1, keepdims=True)
            pv = jnp.dot(p.astype(BF16), v2, preferred_element_type=F32) / l
            pv = jnp.where(in_head, pv, 0.0)
            out = pv if out is None else out + pv
        o_ref[:, sl] = out.astype(BF16)


def attention_tables(rpb):
    h = rpb.shape[0]
    wk = 3 * ATT_ROWS
    idx = np.clip(np.arange(2 * GRID_W - 1) - (GRID_W - NA_KW), 0, 2 * NA_KW - 2)
    v = jnp.concatenate([rpb[:, :, idx], jnp.zeros(rpb.shape[:2] + (1,), rpb.dtype)], axis=-1)
    m = jnp.tile(v, (1, 1, GRID_W))[:, :, :GRID_W * (2 * GRID_W - 1)]
    blk = m.reshape(h, 2 * NA_KH - 1, GRID_W, 2 * GRID_W - 1)[:, :, :, GRID_W - 1:]
    lo = NA_KH - 1 - ATT_ROWS
    per_row = jnp.stack([blk[:, lo - iq: lo - iq + wk] for iq in range(ATT_ROWS)], axis=1)
    bias = jnp.transpose(per_row, (0, 1, 3, 2, 4)).reshape(h, ATT_ROWS * GRID_W, wk * GRID_W).astype(F32)

    iq = np.arange(ATT_ROWS)[:, None, None, None]
    c = np.arange(GRID_W)[None, :, None, None]
    kr = np.arange(wk)[None, None, :, None]
    kc = np.arange(GRID_W)[None, None, None, :]
    shape = (ATT_ROWS, GRID_W, wk, GRID_W)
    ws = np.clip(c - NA_KW // 2, 0, GRID_W - NA_KW)
    col_ok = (kc >= ws) & (kc < ws + NA_KW)
    row_ok = [
        (kr >= ATT_ROWS) & (kr < ATT_ROWS + NA_KH) & (iq >= 0),
        (kr >= iq) & (kr < iq + NA_KH),
        (kr < NA_KH) & (iq >= 0),
    ]
    mask = np.stack([np.where(np.broadcast_to(r & col_ok, shape), 0.0, NEG_INF).reshape(ATT_ROWS * GRID_W, -1)
                     for r in row_ok]).astype(np.float32)
    return bias, jnp.asarray(mask)


def attention_core(qkv, tables, batch, seq):
    n = qkv.shape[0]
    d = N_HEADS * HEAD_DIM
    rows = seq // GRID_W
    groups = rows // ATT_ROWS
    assert rows % ATT_ROWS == 0 and rows >= 4 * ATT_ROWS and NA_KH == 2 * ATT_ROWS
    rq = ATT_ROWS * GRID_W
    bias, mask = tables

    def blk(off, col):
        def index_map(b, g):
            return (b * groups + jnp.clip(g + off, 0, groups - 1), col)
        return pl.BlockSpec((rq, d), index_map)

    def case_map(b, g):
        return (jnp.where(g == 0, 0, jnp.where(g == groups - 1, 2, 1)), 0, 0)

    return pl.pallas_call(
        _attn_kernel,
        grid=(batch, groups),
        in_specs=[blk(0, 0), blk(-1, 1), blk(0, 1), blk(1, 1), blk(-1, 2), blk(0, 2), blk(1, 2),
                  pl.BlockSpec((N_HEADS, rq, 3 * rq), lambda b, g: (0, 0, 0)),
                  pl.BlockSpec((1, rq, 3 * rq), case_map)],
        out_specs=pl.BlockSpec((rq, d), lambda b, g: (b * groups + g, 0)),
        out_shape=jax.ShapeDtypeStruct((n, d), BF16),
        compiler_params=_cparams(("parallel", "parallel")),
    )(qkv, qkv, qkv, qkv, qkv, qkv, qkv, bias, mask)


def _post_kernel(a_ref, x_ref, wo_ref, g_ref, wr_ref, xo_ref, h_ref, p_ref):
    xn = x_ref[...] + jnp.dot(a_ref[...], wo_ref[...], preferred_element_type=F32)
    xo_ref[...] = xn
    hf = _rms(xn, g_ref[...])
    h_ref[...] = hf.astype(BF16)
    logits = lax.dot_general(wr_ref[...], hf, (((1,), (1,)), ((), ())),
                             precision=lax.Precision.HIGHEST, preferred_element_type=F32)
    m = jnp.max(logits, axis=0, keepdims=True)
    e = jnp.exp(logits - m)
    p_ref[...] = e / jnp.sum(e, axis=0, keepdims=True)


def post_mixer(a, x, w_o, g_ffn, w_router_t):
    n, d = x.shape
    e = w_router_t.shape[0]
    return pl.pallas_call(
        _post_kernel,
        grid=(n // ROW_TILE,),
        in_specs=[pl.BlockSpec((ROW_TILE, d), lambda i: (i, 0)),
                  pl.BlockSpec((ROW_TILE, d), lambda i: (i, 0)),
                  pl.BlockSpec((d, d), lambda i: (0, 0)),
                  pl.BlockSpec((1, d), lambda i: (0, 0)),
                  pl.BlockSpec((e, d), lambda i: (0, 0))],
        out_specs=[pl.BlockSpec((ROW_TILE, d), lambda i: (i, 0)),
                   pl.BlockSpec((ROW_TILE, d), lambda i: (i, 0)),
                   pl.BlockSpec((e, ROW_TILE), lambda i: (0, i))],
        out_shape=[jax.ShapeDtypeStruct((n, d), F32),
                   jax.ShapeDtypeStruct((n, d), BF16),
                   jax.ShapeDtypeStruct((e, n), F32)],
        compiler_params=_cparams(("parallel",)),
    )(a, x, w_o, g_ffn.reshape(1, d), w_router_t)


def _threshold_kernel(p_ref, thr_ref, need_ref, *, cap):
    e = p_ref.shape[0]

    def count_ge(cand):
        bits = pltpu.bitcast(p_ref[...], I32)
        return jnp.sum((bits >= cand).astype(F32), axis=1, keepdims=True)

    def body(i, prefix):
        cand = prefix | jnp.left_shift(jnp.int32(1), 30 - i)
        return jnp.where(count_ge(cand) >= cap, cand, prefix)

    thr = lax.fori_loop(0, 31, body, jnp.zeros((e, 1), I32))
    n_gt = count_ge(thr + 1)
    thr_ref[...] = jnp.broadcast_to(thr, thr_ref.shape)
    need_ref[...] = jnp.broadcast_to(cap - n_gt, need_ref.shape)


def _select_kernel(p_ref, thr_ref, need_ref, ut_ref, gate_ref, pos_ref, cnt_ref, carry_ref):
    @pl.when(pl.program_id(0) == 0)
    def _():
        carry_ref[...] = jnp.zeros_like(carry_ref)

    p = p_ref[...]
    bits = pltpu.bitcast(p, I32)
    thr = thr_ref[:, :1]
    gt = bits > thr
    eq = bits == thr
    ut = ut_ref[...]
    eq_f = jnp.where(eq, 1.0, 0.0)
    eq_before = jnp.dot(eq_f.astype(BF16), ut, preferred_element_type=F32) + carry_ref[:, :1]
    sel = gt | (eq & (eq_before < need_ref[:, :1]))
    carry_ref[...] = carry_ref[...] + jnp.sum(eq_f, axis=1, keepdims=True)
    sel_f = jnp.where(sel, 1.0, 0.0)
    rank = jnp.dot(sel_f.astype(BF16), ut, preferred_element_type=F32)
    pos_ref[...] = jnp.where(sel, rank.astype(I32), -1)
    gate_ref[...] = jnp.where(sel, p, 0.0)
    cnt = jnp.sum(sel_f, axis=1, keepdims=True).astype(I32)
    cnt_ref[0] = jnp.broadcast_to(cnt, cnt_ref.shape[1:])


def expert_choice_select(probs_t):
    e, n = probs_t.shape
    cap = EC_CAPACITY * n // e
    j = n // MOE_T
    thr, need = pl.pallas_call(
        functools.partial(_threshold_kernel, cap=cap),
        out_shape=[jax.ShapeDtypeStruct((e, 128), I32), jax.ShapeDtypeStruct((e, 128), F32)],
        compiler_params=pltpu.CompilerParams(vmem_limit_bytes=VMEM_LIMIT),
    )(probs_t)
    it = np.arange(MOE_T)
    ut = jnp.asarray((it[:, None] < it[None, :]).astype(np.float32), BF16)
    gates, pos, cnt = pl.pallas_call(
        _select_kernel,
        grid=(j,),
        in_specs=[pl.BlockSpec((e, MOE_T), lambda i: (0, i)),
                  pl.BlockSpec((e, 128), lambda i: (0, 0)),
                  pl.BlockSpec((e, 128), lambda i: (0, 0)),
                  pl.BlockSpec((MOE_T, MOE_T), lambda i: (0, 0))],
        out_specs=[pl.BlockSpec((e, MOE_T), lambda i: (0, i)),
                   pl.BlockSpec((e, MOE_T), lambda i: (0, i)),
                   pl.BlockSpec((1, e, 128), lambda i: (i, 0, 0))],
        out_shape=[jax.ShapeDtypeStruct((e, n), F32),
                   jax.ShapeDtypeStruct((e, n), I32),
                   jax.ShapeDtypeStruct((j, e, 128), I32)],
        scratch_shapes=[pltpu.VMEM((e, 128), F32)],
        compiler_params=_cparams(("arbitrary",)),
    )(probs_t, thr, need, ut)
    return gates, pos, cnt[:, :, 0].T


def _one_hot_slots(pos_ref, row_of_rank0, lo, hi):
    e, t = pos_ref.shape
    rows = lax.broadcasted_iota(I32, (MOE_W, t), 0)
    blocks = []
    for i in range(e):
        r = pos_ref[i:i + 1, :]
        tgt = jnp.where((r >= jnp.maximum(lo[i], 0)) & (r < hi[i]), r + row_of_rank0[i], -1)
        blocks.append(jnp.where(rows == tgt, 1.0, 0.0).astype(BF16))
    return jnp.concatenate(blocks, axis=0)


def _gather_kernel(s16_ref, off_ref, cpass_ref, coff_ref, npass_ref, h_ref, gate_ref, pos_ref, xs_ref,
                   ybuf, carry, zbuf, nbatch, sem, *, region, cap):
    j = pl.program_id(0)
    nj = pl.num_programs(0)
    e, t = pos_ref.shape
    d = h_ref.shape[1]

    @pl.when(j == 0)
    def _():
        nbatch[0] = 0
        carry[...] = jnp.zeros_like(carry)

    g = gate_ref[...]
    g_hi = g.astype(BF16)
    r1 = g - g_hi.astype(F32)
    g_mid = r1.astype(BF16)
    g_lo = (r1 - g_mid.astype(F32)).astype(BF16)
    g3 = jnp.concatenate([g_hi, g_mid, g_lo, jnp.zeros((GATE_COLS - 3 * e, t), BF16)], axis=0)

    def window_copy(slot, i, dst):
        return pltpu.make_async_copy(ybuf.at[slot, pl.ds(i * MOE_W, MOE_W)], xs_ref.at[pl.ds(dst, MOE_W)], sem.at[0])

    def wait_batch(slot):
        for i in range(e):
            window_copy(slot, i, 0).wait()

    def one_pass(p, carry_in):
        k = nbatch[0]
        slot = k % 2
        off = [off_ref[i * nj + j] for i in range(e)]
        shift = [off[i] - p * MOE_W for i in range(e)]
        onehot = _one_hot_slots(pos_ref, shift, [-shift[i] for i in range(e)], [MOE_W - shift[i] for i in range(e)])
        ybuf[slot, :, :d] = jnp.dot(onehot, h_ref[...], preferred_element_type=F32).astype(BF16)
        ybuf[slot, :, d:] = lax.dot_general(onehot, g3, (((1,), (1,)), ((), ())),
                                            preferred_element_type=F32).astype(BF16)
        for i in range(e):
            @pl.when((p == 0) & (off[i] > 0))
            def _():
                rows = pl.ds(i * MOE_W, SUB)
                ybuf[slot, rows, :] = (ybuf[slot, rows, :].astype(F32)
                                       + carry[pl.ds(i * SUB, SUB), :].astype(F32)).astype(BF16)

            @pl.when(p == cpass_ref[i * nj + j])
            def _():
                src = pl.multiple_of(i * MOE_W + coff_ref[i * nj + j], SUB)
                carry[pl.ds(i * SUB, SUB), :] = ybuf[slot, pl.ds(src, SUB), :]

        @pl.when(k > 0)
        def _():
            wait_batch(1 - slot)

        for i in range(e):
            dst = pl.multiple_of(i * region + s16_ref[i * nj + j] + p * MOE_W, SUB)
            window_copy(slot, i, dst).start()
        nbatch[0] = k + 1
        return carry_in

    lax.fori_loop(0, npass_ref[j], one_pass, 0)

    @pl.when(j == nj - 1)
    def _():
        wait_batch((nbatch[0] - 1) % 2)
        zbuf[...] = jnp.zeros_like(zbuf)
        zr = zbuf.shape[0]
        cps = [pltpu.make_async_copy(zbuf, xs_ref.at[pl.ds(i * region + cap + c * zr, zr)], sem.at[1])
               for i in range(e) for c in range((region - cap) // zr)]
        for cp in cps:
            cp.start()
        for cp in cps:
            cp.wait()


def _ffn_kernel(x_ref, wg_ref, wu_ref, wd_ref, o_ref, acc_ref):
    ei, f = pl.program_id(0), pl.program_id(2)
    d = o_ref.shape[1]

    @pl.when(f == 0)
    def _():
        acc_ref[...] = jnp.zeros_like(acc_ref)

    x = x_ref[:, :d]
    gate = jnp.dot(x, wg_ref[0], preferred_element_type=F32)
    up = jnp.dot(x, wu_ref[0], preferred_element_type=F32)
    hid = (gate * jax.nn.sigmoid(gate) * up).astype(BF16)
    acc_ref[...] += jnp.dot(hid, wd_ref[0], preferred_element_type=F32)

    @pl.when(f == pl.num_programs(2) - 1)
    def _():
        n_e = pl.num_programs(0)
        lane = lax.broadcasted_iota(I32, (1, GATE_COLS), 1)
        own = (lane < 3 * n_e) & (lane % n_e == ei)
        g = jnp.sum(jnp.where(own, x_ref[:, d:].astype(F32), 0.0), axis=1, keepdims=True)
        o_ref[...] = (acc_ref[...] * g).astype(BF16)


def _combine_kernel(a_ref, s16_ref, npass_ref, x_ref, pos_ref, gfin_ref, o_hbm, out_ref, obuf, sem,
                    *, region, cap, final_norm):
    j = pl.program_id(0)
    nj = pl.num_programs(0)
    e = pos_ref.shape[0]
    slot = j % 2

    def window_start(i, jj, p):
        return jnp.minimum(s16_ref[i * nj + jj] + p * MOE_W, cap - MOE_W)

    def window_copy(i, src, buf):
        return pltpu.make_async_copy(o_hbm.at[pl.ds(src, MOE_W)], obuf.at[buf, pl.ds(i * MOE_W, MOE_W)], sem.at[buf])

    def fetch(jj, p, buf):
        for i in range(e):
            window_copy(i, pl.multiple_of(i * region + window_start(i, jj, p), SUB), buf).start()

    def wait(buf):
        for i in range(e):
            window_copy(i, 0, buf).wait()

    def one_hot(p):
        lo, hi, row0 = [], [], []
        for i in range(e):
            a = a_ref[i * nj + j]
            first = s16_ref[i * nj + j] + p * MOE_W - a
            lo.append(first)
            hi.append(first + MOE_W)
            row0.append(a - window_start(i, j, p))
        return _one_hot_slots(pos_ref, row0, lo, hi)

    @pl.when(j == 0)
    def _():
        fetch(0, 0, 0)

    @pl.when(j + 1 < nj)
    def _():
        fetch(j + 1, 0, 1 - slot)

    onehot = one_hot(0)
    wait(slot)
    y = x_ref[...] + lax.dot_general(onehot, obuf[slot], (((0,), (0,)), ((), ())), preferred_element_type=F32)

    def extra_pass(p, acc):
        fetch(j, p, 2)
        oh = one_hot(p)
        wait(2)
        return acc + lax.dot_general(oh, obuf[2], (((0,), (0,)), ((), ())), preferred_element_type=F32)

    y = lax.fori_loop(1, npass_ref[j], extra_pass, y)
    out_ref[...] = _rms(y, gfin_ref[...]) if final_norm else y


def _largest_tile(n, options):
    for t in options:
        if n % t == 0:
            return t
    raise ValueError(f"no tile in {options} divides {n}")


def expert_choice_moe(x, h, probs_t, w_gate, w_up, w_down, g_final=None):
    n, d = x.shape
    e = probs_t.shape[0]
    f = w_gate.shape[2]
    cap = EC_CAPACITY * n // e
    nj = n // MOE_T
    da = d + GATE_COLS
    assert 3 * e <= GATE_COLS and cap % SUB == 0 and cap >= MOE_W
    gates, pos, cnt = expert_choice_select(probs_t)

    ends = jnp.cumsum(cnt, axis=1)
    a = ends - cnt
    off = a % SUB
    s16 = a - off
    tot = off + cnt
    group = tot // SUB * SUB
    cpass = jnp.where(tot % SUB != 0, group // MOE_W, -1)
    coff = group % MOE_W
    npass = jnp.maximum(1, jnp.max((tot + MOE_W - 1) // MOE_W, axis=0))
    flat = lambda v: v.astype(I32).reshape(-1)

    tm = _largest_tile(cap, (1024, 512, 256, 128))
    zr = 256
    region = cap + -(-(MOE_T + 2 * MOE_W) // max(tm, zr)) * max(tm, zr)
    mt = cap // tm

    xs = pl.pallas_call(
        functools.partial(_gather_kernel, region=region, cap=cap),
        grid_spec=pltpu.PrefetchScalarGridSpec(
            num_scalar_prefetch=5,
            grid=(nj,),
            in_specs=[pl.BlockSpec((MOE_T, d), lambda i, *_: (i, 0)),
                      pl.BlockSpec((e, MOE_T), lambda i, *_: (0, i)),
                      pl.BlockSpec((e, MOE_T), lambda i, *_: (0, i))],
            out_specs=pl.BlockSpec(memory_space=pl.ANY),
            scratch_shapes=[pltpu.VMEM((2, e * MOE_W, da), BF16),
                            pltpu.VMEM((e * SUB, da), BF16),
                            pltpu.VMEM((zr, da), BF16),
                            pltpu.SMEM((1,), I32),
                            pltpu.SemaphoreType.DMA((2,))]),
        out_shape=jax.ShapeDtypeStruct((e * region, da), BF16),
        compiler_params=_cparams(("arbitrary",)),
    )(flat(s16), flat(off), flat(cpass), flat(coff), npass.astype(I32), h, gates, pos)

    fc = f // 2 if (f > 1024 and (f // 2) % 128 == 0) else f
    ft = f // fc
    rt = region // tm
    out_slots = pl.pallas_call(
        _ffn_kernel,
        grid=(e, mt, ft),
        in_specs=[pl.BlockSpec((tm, da), lambda ei, m, fi: (ei * rt + m, 0)),
                  pl.BlockSpec((1, d, fc), lambda ei, m, fi: (ei, 0, fi)),
                  pl.BlockSpec((1, d, fc), lambda ei, m, fi: (ei, 0, fi)),
                  pl.BlockSpec((1, fc, d), lambda ei, m, fi: (ei, fi, 0))],
        out_specs=pl.BlockSpec((tm, d), lambda ei, m, fi: (ei * mt + m, 0)),
        out_shape=jax.ShapeDtypeStruct((e * cap, d), BF16),
        scratch_shapes=[pltpu.VMEM((tm, d), F32)],
        compiler_params=_cparams(("parallel", "parallel", "arbitrary")),
    )(xs, w_gate, w_up, w_down)

    final_norm = g_final is not None
    gfin = (g_final if final_norm else jnp.ones((d,), F32)).reshape(1, d)
    return pl.pallas_call(
        functools.partial(_combine_kernel, region=cap, cap=cap, final_norm=final_norm),
        grid_spec=pltpu.PrefetchScalarGridSpec(
            num_scalar_prefetch=3,
            grid=(nj,),
            in_specs=[pl.BlockSpec((MOE_T, d), lambda i, *_: (i, 0)),
                      pl.BlockSpec((e, MOE_T), lambda i, *_: (0, i)),
                      pl.BlockSpec((1, d), lambda i, *_: (0, 0)),
                      pl.BlockSpec(memory_space=pl.ANY)],
            out_specs=pl.BlockSpec((MOE_T, d), lambda i, *_: (i, 0)),
            scratch_shapes=[pltpu.VMEM((3, e * MOE_W, d), BF16),
                            pltpu.SemaphoreType.DMA((3,))]),
        out_shape=jax.ShapeDtypeStruct((n, d), F32),
        compiler_params=_cparams(("arbitrary",)),
    )(flat(a), flat(s16), npass.astype(I32), x, pos, gfin, out_slots)


def _trunk(x3, params):
    batch, seq, d = x3.shape
    x = x3.reshape(batch * seq, d)
    depth = params["norm_mix"].shape[0]
    for i in range(depth):
        jm = i // 2
        if i % 2 == 0:
            a = fourier_core(x, params["norm_mix"][i], params["w_fourier_in"][jm], batch, seq)
            w_o = params["w_fourier_out"][jm]
        else:
            qkv = norm_matmul(x, params["norm_mix"][i], params["w_qkv"][jm])
            a = attention_core(qkv, params["attn_tables"][jm], batch, seq)
            w_o = params["w_attn_out"][jm]
        x, h, probs_t = post_mixer(a, x, w_o, params["norm_ffn"][i], params["w_router_t"][i])
        g_final = params["norm_final"] if i == depth - 1 else None
        x = expert_choice_moe(x, h, probs_t, params["w_gate"][i], params["w_up"][i], params["w_down"][i], g_final)
    return x.reshape(batch, seq, d)


def kernel(x_prompt, x_sample, norm_mix, norm_ffn, norm_final, w_fourier_in, w_fourier_out, w_qkv, w_attn_out,
           rel_pos_bias, w_router, w_gate, w_up, w_down):
    params = dict(
        norm_mix=norm_mix, norm_ffn=norm_ffn, norm_final=norm_final,
        attn_tables=[attention_tables(rel_pos_bias[i]) for i in range(rel_pos_bias.shape[0])],
        w_fourier_in=w_fourier_in.astype(BF16), w_fourier_out=w_fourier_out.astype(BF16),
        w_qkv=w_qkv.astype(BF16), w_attn_out=w_attn_out.astype(BF16),
        w_router_t=jnp.swapaxes(w_router, 1, 2),
        w_gate=w_gate.astype(BF16), w_up=w_up.astype(BF16), w_down=w_down.astype(BF16))
    return (_trunk(x_prompt, params), _trunk(x_sample, params))
```

```python
import functools
import math

import numpy as np
import jax
import jax.numpy as jnp
from jax import lax
from jax.experimental import pallas as pl
from jax.experimental.pallas import tpu as pltpu

EPS = 1e-6
GRID_W = 64
N_HEADS = 16
HEAD_DIM = 64
NA_KH = 8
NA_KW = 16
F_GROUPS = 4
EC_CAPACITY = 2
NEG_INF = -1e30

ROW_TILE = 512
ATT_ROWS = 4
MOE_T = 256
MOE_W = 64
SUB = 16
LANES = 128
GATE_COLS = 128
VMEM_LIMIT = 56 * 1024 * 1024

F32 = jnp.float32
BF16 = jnp.bfloat16
I32 = jnp.int32


def _cparams(sem):
    return pltpu.CompilerParams(dimension_semantics=sem, vmem_limit_bytes=VMEM_LIMIT)


def _rms(x, g):
    r = lax.rsqrt(jnp.mean(x * x, axis=-1, keepdims=True) + EPS)
    return (x * r) * g


def _norm_matmul_kernel(x_ref, g_ref, w_ref, o_ref, *, col_chunk):
    h = _rms(x_ref[...], g_ref[...]).astype(BF16)
    for c in range(o_ref.shape[1] // col_chunk):
        sl = slice(c * col_chunk, (c + 1) * col_chunk)
        o_ref[:, sl] = jnp.dot(h, w_ref[:, sl], preferred_element_type=F32).astype(o_ref.dtype)


def norm_matmul(x, g, w):
    n, d = x.shape
    m = w.shape[1]
    return pl.pallas_call(
        functools.partial(_norm_matmul_kernel, col_chunk=min(m, 1024)),
        grid=(n // ROW_TILE,),
        in_specs=[pl.BlockSpec((ROW_TILE, d), lambda i: (i, 0)),
                  pl.BlockSpec((1, d), lambda i: (0, 0)),
                  pl.BlockSpec((d, m), lambda i: (0, 0))],
        out_specs=pl.BlockSpec((ROW_TILE, m), lambda i: (i, 0)),
        out_shape=jax.ShapeDtypeStruct((n, m), BF16),
        compiler_params=_cparams(("parallel",)),
    )(x, g.reshape(1, d), w)


def _fourier_in_kernel(x_ref, g_ref, w_ref, cs_ref, vr_ref, vi_ref):
    h = _rms(x_ref[...], g_ref[...]).astype(BF16)
    u = jnp.dot(h, w_ref[...], preferred_element_type=F32).astype(BF16)
    gd = cs_ref.shape[0]
    for gi in range(u.shape[1] // gd):
        sl = slice(gi * gd, (gi + 1) * gd)
        ab = jnp.dot(u[:, sl], cs_ref[...], preferred_element_type=F32)
        vr_ref[:, sl] = ab[:, :gd].astype(BF16)
        vi_ref[:, sl] = ab[:, gd:].astype(BF16)


def _dft_a_kernel(vr_ref, vi_ref, m_ref, yr_ref, yi_ref, in_r, in_i, out_r, out_i):
    n1, t2, dc = vr_ref.shape
    chunks = [slice(c * LANES, (c + 1) * LANES) for c in range(dc // LANES)]
    for c, sl in enumerate(chunks):
        in_r[c] = vr_ref[:, :, sl].astype(F32).reshape(n1 * t2, LANES)
        in_i[c] = vi_ref[:, :, sl].astype(F32).reshape(n1 * t2, LANES)
    for s in range(t2):
        rows = pl.ds(s, n1, stride=t2)
        v = jnp.concatenate([jnp.concatenate([ref[c, rows, :] for c in range(len(chunks))], axis=1)
                             for ref in (in_r, in_i)], axis=0).astype(BF16)
        y = jnp.dot(m_ref[...], v, preferred_element_type=F32)
        for c, sl in enumerate(chunks):
            out_r[c, rows, :] = y[:n1, sl]
            out_i[c, rows, :] = y[n1:, sl]
    for c, sl in enumerate(chunks):
        yr_ref[:, :, sl] = out_r[c].reshape(n1, t2, LANES).astype(BF16)
        yi_ref[:, :, sl] = out_i[c].reshape(n1, t2, LANES).astype(BF16)


def _dft_c_kernel(yr_ref, yi_ref, g_ref, z_ref, out_z):
    kb, n2, d = yr_ref.shape
    chunks = [slice(c * LANES, (c + 1) * LANES) for c in range(d // LANES)]
    for i in range(kb):
        ycat = jnp.concatenate([yr_ref[i], yi_ref[i]], axis=0)
        z = jnp.dot(g_ref[i], ycat, preferred_element_type=F32)
        for c, sl in enumerate(chunks):
            out_z[c, pl.ds(i, n2, stride=kb), :] = z[:, sl]
    for c, sl in enumerate(chunks):
        z_ref[:, :, sl] = out_z[c].reshape(n2, kb, LANES).astype(BF16)


def _dft_tables(s, n1, n2, gd):
    def cs(num, den):
        ang = (2.0 * math.pi / den) * (num % den).astype(F32)
        return jnp.cos(ang), jnp.sin(ang)

    ic = jnp.arange(gd, dtype=I32)
    cc, sc = cs(ic[:, None] * ic[None, :], gd)
    chan = jnp.concatenate([cc, -sc], axis=1).astype(BF16)
    i1 = jnp.arange(n1, dtype=I32)
    c1, s1 = cs(i1[:, None] * i1[None, :], n1)
    m1 = jnp.concatenate([jnp.concatenate([c1, s1], axis=1),
                          jnp.concatenate([-s1, c1], axis=1)], axis=0).astype(BF16)
    k1 = jnp.arange(n1, dtype=I32)[:, None, None]
    k2 = jnp.arange(n2, dtype=I32)[None, :, None]
    s2 = jnp.arange(n2, dtype=I32)[None, None, :]
    cg, sg = cs(((n1 * k2 + k1) % s) * s2, s)
    scale = 1.0 / math.sqrt(float(s) * float(gd))
    g = (jnp.concatenate([cg, sg], axis=2) * scale).astype(BF16)
    return chan, m1, g


def fourier_core(x, g_norm, w_in, batch, seq):
    n, d = x.shape
    gd = d // F_GROUPS
    n1 = 1 << ((seq.bit_length() - 1 + 1) // 2)
    n2 = seq // n1
    assert n1 * n2 == seq and n1 % 16 == 0 and n2 % 16 == 0
    chan, m1, gtab = _dft_tables(seq, n1, n2, gd)

    vr, vi = pl.pallas_call(
        _fourier_in_kernel,
        grid=(n // ROW_TILE,),
        in_specs=[pl.BlockSpec((ROW_TILE, d), lambda i: (i, 0)),
                  pl.BlockSpec((1, d), lambda i: (0, 0)),
                  pl.BlockSpec((d, d), lambda i: (0, 0)),
                  pl.BlockSpec((gd, 2 * gd), lambda i: (0, 0))],
        out_specs=[pl.BlockSpec((ROW_TILE, d), lambda i: (i, 0))] * 2,
        out_shape=[jax.ShapeDtypeStruct((n, d), BF16)] * 2,
        compiler_params=_cparams(("parallel",)),
    )(x, g_norm.reshape(1, d), w_in, chan)

    t2, dc = SUB, min(d, 512)
    a_spec = pl.BlockSpec((n1, t2, dc), lambda b, j, c: (b, j, c))
    yr, yi = pl.pallas_call(
        _dft_a_kernel,
        grid=(batch, n2 // t2, d // dc),
        in_specs=[a_spec, a_spec, pl.BlockSpec((2 * n1, 2 * n1), lambda b, j, c: (0, 0))],
        out_specs=[a_spec, a_spec],
        out_shape=[jax.ShapeDtypeStruct((batch * n1, n2, d), BF16)] * 2,
        scratch_shapes=[pltpu.VMEM((dc // LANES, n1 * t2, LANES), F32) for _ in range(4)],
        compiler_params=_cparams(("parallel", "parallel", "parallel")),
    )(vr.reshape(batch * n1, n2, d), vi.reshape(batch * n1, n2, d), m1)

    kb = SUB
    c_spec = pl.BlockSpec((kb, n2, d), lambda b, k: (b * (n1 // kb) + k, 0, 0))
    z = pl.pallas_call(
        _dft_c_kernel,
        grid=(batch, n1 // kb),
        in_specs=[c_spec, c_spec, pl.BlockSpec((kb, n2, 2 * n2), lambda b, k: (k, 0, 0))],
        out_specs=pl.BlockSpec((n2, kb, d), lambda b, k: (b, k, 0)),
        out_shape=jax.ShapeDtypeStruct((batch * n2, n1, d), BF16),
        scratch_shapes=[pltpu.VMEM((d // LANES, n2 * kb, LANES), F32)],
        compiler_params=_cparams(("parallel", "parallel")),
    )(yr, yi, gtab)
    return z.reshape(n, d)


def _attn_kernel(q_ref, kp_ref, kc_ref, kn_ref, vp_ref, vc_ref, vn_ref, bias_ref, mask_ref, o_ref):
    q = q_ref[...] * jnp.asarray(HEAD_DIM ** -0.5, BF16)
    k = jnp.concatenate([kp_ref[...], kc_ref[...], kn_ref[...]], axis=0)
    v = jnp.concatenate([vp_ref[...], vc_ref[...], vn_ref[...]], axis=0)
    mask = mask_ref[0]
    lane = lax.broadcasted_iota(I32, (1, 2 * HEAD_DIM), 1)
    for hp in range(N_HEADS // 2):
        sl = slice(hp * 2 * HEAD_DIM, (hp + 1) * 2 * HEAD_DIM)
        q2, k2, v2 = q[:, sl], k[:, sl], v[:, sl]
        out = None
        for s in range(2):
            in_head = (lane >= s * HEAD_DIM) & (lane < (s + 1) * HEAD_DIM)
            qm = jnp.where(in_head, q2, jnp.zeros_like(q2))
            sc = lax.dot_general(qm, k2, (((1,), (1,)), ((), ())), preferred_element_type=F32)
            sc = sc + bias_ref[2 * hp + s] + mask
            m = jnp.max(sc, axis=1, keepdims=True)
            p = jnp.exp(sc - m)
            l = jnp.sum(p, axis=1, keepdims=True)
            pv = jnp.dot(p.astype(BF16), v2, preferred_element_type=F32) / l
            pv = jnp.where(in_head, pv, 0.0)
            out = pv if out is None else out + pv
        o_ref[:, sl] = out.astype(BF16)


def attention_tables(rpb):
    h = rpb.shape[0]
    wk = 3 * ATT_ROWS
    idx = np.clip(np.arange(2 * GRID_W - 1) - (GRID_W - NA_KW), 0, 2 * NA_KW - 2)
    v = jnp.concatenate([rpb[:, :, idx], jnp.zeros(rpb.shape[:2] + (1,), rpb.dtype)], axis=-1)
    m = jnp.tile(v, (1, 1, GRID_W))[:, :, :GRID_W * (2 * GRID_W - 1)]
    blk = m.reshape(h, 2 * NA_KH - 1, GRID_W, 2 * GRID_W - 1)[:, :, :, GRID_W - 1:]
    lo = NA_KH - 1 - ATT_ROWS
    per_row = jnp.stack([blk[:, lo - iq: lo - iq + wk] for iq in range(ATT_ROWS)], axis=1)
    bias = jnp.transpose(per_row, (0, 1, 3, 2, 4)).reshape(h, ATT_ROWS * GRID_W, wk * GRID_W).astype(F32)

    iq = np.arange(ATT_ROWS)[:, None, None, None]
    c = np.arange(GRID_W)[None, :, None, None]
    kr = np.arange(wk)[None, None, :, None]
    kc = np.arange(GRID_W)[None, None, None, :]
    shape = (ATT_ROWS, GRID_W, wk, GRID_W)
    ws = np.clip(c - NA_KW // 2, 0, GRID_W - NA_KW)
    col_ok = (kc >= ws) & (kc < ws + NA_KW)
    row_ok = [
        (kr >= ATT_ROWS) & (kr < ATT_ROWS + NA_KH) & (iq >= 0),
        (kr >= iq) & (kr < iq + NA_KH),
        (kr < NA_KH) & (iq >= 0),
    ]
    mask = np.stack([np.where(np.broadcast_to(r & col_ok, shape), 0.0, NEG_INF).reshape(ATT_ROWS * GRID_W, -1)
                     for r in row_ok]).astype(np.float32)
    return bias, jnp.asarray(mask)


def attention_core(qkv, tables, batch, seq):
    n = qkv.shape[0]
    d = N_HEADS * HEAD_DIM
    rows = seq // GRID_W
    groups = rows // ATT_ROWS
    assert rows % ATT_ROWS == 0 and rows >= 4 * ATT_ROWS and NA_KH == 2 * ATT_ROWS
    rq = ATT_ROWS * GRID_W
    bias, mask = tables

    def blk(off, col):
        def index_map(b, g):
            return (b * groups + jnp.clip(g + off, 0, groups - 1), col)
        return pl.BlockSpec((rq, d), index_map)

    def case_map(b, g):
        return (jnp.where(g == 0, 0, jnp.where(g == groups - 1, 2, 1)), 0, 0)

    return pl.pallas_call(
        _attn_kernel,
        grid=(batch, groups),
        in_specs=[blk(0, 0), blk(-1, 1), blk(0, 1), blk(1, 1), blk(-1, 2), blk(0, 2), blk(1, 2),
                  pl.BlockSpec((N_HEADS, rq, 3 * rq), lambda b, g: (0, 0, 0)),
                  pl.BlockSpec((1, rq, 3 * rq), case_map)],
        out_specs=pl.BlockSpec((rq, d), lambda b, g: (b * groups + g, 0)),
        out_shape=jax.ShapeDtypeStruct((n, d), BF16),
        compiler_params=_cparams(("parallel", "parallel")),
    )(qkv, qkv, qkv, qkv, qkv, qkv, qkv, bias, mask)


def _post_kernel(a_ref, x_ref, wo_ref, g_ref, wr_ref, xo_ref, h_ref, p_ref):
    for r in range(x_ref.shape[0] // LANES):
        rows = slice(r * LANES, (r + 1) * LANES)
        xn = x_ref[rows, :] + jnp.dot(a_ref[rows, :], wo_ref[...], preferred_element_type=F32)
        xo_ref[rows, :] = xn
        hf = _rms(xn, g_ref[...])
        h_hi = hf.astype(BF16)
        h_ref[rows, :] = h_hi
        h_lo = (hf - h_hi.astype(F32)).astype(BF16)
        lg = (jnp.dot(h_hi, wr_ref[0], preferred_element_type=F32)
              + jnp.dot(h_lo, wr_ref[0], preferred_element_type=F32)
              + jnp.dot(h_hi, wr_ref[1], preferred_element_type=F32))
        logits = lg.T[:p_ref.shape[0]]
        m = jnp.max(logits, axis=0, keepdims=True)
        e = jnp.exp(logits - m)
        p_ref[:, rows] = e / jnp.sum(e, axis=0, keepdims=True)


def split_router_weights(w_router):
    d, e = w_router.shape
    hi = w_router.astype(BF16)
    lo = (w_router - hi.astype(F32)).astype(BF16)
    return jnp.pad(jnp.stack([hi, lo]), ((0, 0), (0, 0), (0, LANES - e)))


def post_mixer(a, x, w_o, g_ffn, w_router2, e):
    n, d = x.shape
    return pl.pallas_call(
        _post_kernel,
        grid=(n // ROW_TILE,),
        in_specs=[pl.BlockSpec((ROW_TILE, d), lambda i: (i, 0)),
                  pl.BlockSpec((ROW_TILE, d), lambda i: (i, 0)),
                  pl.BlockSpec((d, d), lambda i: (0, 0)),
                  pl.BlockSpec((1, d), lambda i: (0, 0)),
                  pl.BlockSpec((2, d, LANES), lambda i: (0, 0, 0))],
        out_specs=[pl.BlockSpec((ROW_TILE, d), lambda i: (i, 0)),
                   pl.BlockSpec((ROW_TILE, d), lambda i: (i, 0)),
                   pl.BlockSpec((e, ROW_TILE), lambda i: (0, i))],
        out_shape=[jax.ShapeDtypeStruct((n, d), F32),
                   jax.ShapeDtypeStruct((n, d), BF16),
                   jax.ShapeDtypeStruct((e, n), F32)],
        compiler_params=_cparams(("parallel",)),
    )(a, x, w_o, g_ffn.reshape(1, d), w_router2)


def _threshold_kernel(p_ref, thr_ref, need_ref, *, cap):
    e = p_ref.shape[0]

    def count_ge(cand):
        bits = pltpu.bitcast(p_ref[...], I32)
        return jnp.sum((bits >= cand).astype(F32), axis=1, keepdims=True)

    def body(i, prefix):
        cand = prefix | jnp.left_shift(jnp.int32(1), 30 - i)
        return jnp.where(count_ge(cand) >= cap, cand, prefix)

    thr = lax.fori_loop(0, 31, body, jnp.zeros((e, 1), I32))
    n_gt = count_ge(thr + 1)
    thr_ref[...] = jnp.broadcast_to(thr, thr_ref.shape)
    need_ref[...] = jnp.broadcast_to(cap - n_gt, need_ref.shape)


def _select_kernel(p_ref, thr_ref, need_ref, ut_ref, gate_ref, pos_ref, cnt_ref, carry_ref):
    @pl.when(pl.program_id(0) == 0)
    def _():
        carry_ref[...] = jnp.zeros_like(carry_ref)

    p = p_ref[...]
    bits = pltpu.bitcast(p, I32)
    thr = thr_ref[:, :1]
    gt = bits > thr
    eq = bits == thr
    ut = ut_ref[...]
    eq_f = jnp.where(eq, 1.0, 0.0)
    eq_before = jnp.dot(eq_f.astype(BF16), ut, preferred_element_type=F32) + carry_ref[:, :1]
    sel = gt | (eq & (eq_before < need_ref[:, :1]))
    carry_ref[...] = carry_ref[...] + jnp.sum(eq_f, axis=1, keepdims=True)
    sel_f = jnp.where(sel, 1.0, 0.0)
    rank = jnp.dot(sel_f.astype(BF16), ut, preferred_element_type=F32)
    pos_ref[...] = jnp.where(sel, rank.astype(I32), -1)
    gate_ref[...] = jnp.where(sel, p, 0.0)
    cnt = jnp.sum(sel_f, axis=1, keepdims=True).astype(I32)
    cnt_ref[0] = jnp.broadcast_to(cnt, cnt_ref.shape[1:])


def expert_choice_select(probs_t):
    e, n = probs_t.shape
    cap = EC_CAPACITY * n // e
    j = n // MOE_T
    thr, need = pl.pallas_call(
        functools.partial(_threshold_kernel, cap=cap),
        out_shape=[jax.ShapeDtypeStruct((e, 128), I32), jax.ShapeDtypeStruct((e, 128), F32)],
        compiler_params=pltpu.CompilerParams(vmem_limit_bytes=VMEM_LIMIT),
    )(probs_t)
    it = np.arange(MOE_T)
    ut = jnp.asarray((it[:, None] < it[None, :]).astype(np.float32), BF16)
    gates, pos, cnt = pl.pallas_call(
        _select_kernel,
        grid=(j,),
        in_specs=[pl.BlockSpec((e, MOE_T), lambda i: (0, i)),
                  pl.BlockSpec((e, 128), lambda i: (0, 0)),
                  pl.BlockSpec((e, 128), lambda i: (0, 0)),
                  pl.BlockSpec((MOE_T, MOE_T), lambda i: (0, 0))],
        out_specs=[pl.BlockSpec((e, MOE_T), lambda i: (0, i)),
                   pl.BlockSpec((e, MOE_T), lambda i: (0, i)),
                   pl.BlockSpec((1, e, 128), lambda i: (i, 0, 0))],
        out_shape=[jax.ShapeDtypeStruct((e, n), F32),
                   jax.ShapeDtypeStruct((e, n), I32),
                   jax.ShapeDtypeStruct((j, e, 128), I32)],
        scratch_shapes=[pltpu.VMEM((e, 128), F32)],
        compiler_params=_cparams(("arbitrary",)),
    )(probs_t, thr, need, ut)
    return gates, pos, cnt[:, :, 0].T


def _one_hot_slots(pos_ref, row_of_rank0, lo, hi):
    e, t = pos_ref.shape
    rows = lax.broadcasted_iota(I32, (MOE_W, t), 0)
    blocks = []
    for i in range(e):
        r = pos_ref[i:i + 1, :]
        tgt = jnp.where((r >= jnp.maximum(lo[i], 0)) & (r < hi[i]), r + row_of_rank0[i], -1)
        blocks.append(jnp.where(rows == tgt, 1.0, 0.0).astype(BF16))
    return jnp.concatenate(blocks, axis=0)


def _gather_kernel(s16_ref, off_ref, cpass_ref, coff_ref, npass_ref, h_ref, gate_ref, pos_ref, xs_ref,
                   ybuf, carry, zbuf, nbatch, sem, *, region, cap):
    j = pl.program_id(0)
    nj = pl.num_programs(0)
    e, t = pos_ref.shape
    d = h_ref.shape[1]

    @pl.when(j == 0)
    def _():
        nbatch[0] = 0
        carry[...] = jnp.zeros_like(carry)

    g = gate_ref[...]
    g_hi = g.astype(BF16)
    r1 = g - g_hi.astype(F32)
    g_mid = r1.astype(BF16)
    g_lo = (r1 - g_mid.astype(F32)).astype(BF16)
    g3 = jnp.concatenate([g_hi, g_mid, g_lo, jnp.zeros((GATE_COLS - 3 * e, t), BF16)], axis=0)

    def window_copy(slot, i, dst):
        return pltpu.make_async_copy(ybuf.at[slot, pl.ds(i * MOE_W, MOE_W)], xs_ref.at[pl.ds(dst, MOE_W)], sem.at[0])

    def wait_batch(slot):
        for i in range(e):
            window_copy(slot, i, 0).wait()

    def one_pass(p, carry_in):
        k = nbatch[0]
        slot = k % 2
        off = [off_ref[i * nj + j] for i in range(e)]
        shift = [off[i] - p * MOE_W for i in range(e)]
        onehot = _one_hot_slots(pos_ref, shift, [-shift[i] for i in range(e)], [MOE_W - shift[i] for i in range(e)])
        ybuf[slot, :, :d] = jnp.dot(onehot, h_ref[...], preferred_element_type=F32).astype(BF16)
        ybuf[slot, :, d:] = lax.dot_general(onehot, g3, (((1,), (1,)), ((), ())),
                                            preferred_element_type=F32).astype(BF16)
        for i in range(e):
            @pl.when((p == 0) & (off[i] > 0))
            def _():
                rows = pl.ds(i * MOE_W, SUB)
                ybuf[slot, rows, :] = (ybuf[slot, rows, :].astype(F32)
                                       + carry[pl.ds(i * SUB, SUB), :].astype(F32)).astype(BF16)

            @pl.when(p == cpass_ref[i * nj + j])
            def _():
                src = pl.multiple_of(i * MOE_W + coff_ref[i * nj + j], SUB)
                carry[pl.ds(i * SUB, SUB), :] = ybuf[slot, pl.ds(src, SUB), :]

        @pl.when(k > 0)
        def _():
            wait_batch(1 - slot)

        for i in range(e):
            dst = pl.multiple_of(i * region + s16_ref[i * nj + j] + p * MOE_W, SUB)
            window_copy(slot, i, dst).start()
        nbatch[0] = k + 1
        return carry_in

    lax.fori_loop(0, npass_ref[j], one_pass, 0)

    @pl.when(j == nj - 1)
    def _():
        wait_batch((nbatch[0] - 1) % 2)
        zbuf[...] = jnp.zeros_like(zbuf)
        zr = zbuf.shape[0]
        cps = [pltpu.make_async_copy(zbuf, xs_ref.at[pl.ds(i * region + cap + c * zr, zr)], sem.at[1])
               for i in range(e) for c in range((region - cap) // zr)]
        for cp in cps:
            cp.start()
        for cp in cps:
            cp.wait()


def _ffn_kernel(x_ref, wg_ref, wu_ref, wd_ref, o_ref, acc_ref):
    ei, f = pl.program_id(0), pl.program_id(2)
    d = o_ref.shape[1]

    @pl.when(f == 0)
    def _():
        acc_ref[...] = jnp.zeros_like(acc_ref)

    x = x_ref[:, :d]
    gate = jnp.dot(x, wg_ref[0], preferred_element_type=F32)
    up = jnp.dot(x, wu_ref[0], preferred_element_type=F32)
    hid = (gate * jax.nn.sigmoid(gate) * up).astype(BF16)
    acc_ref[...] += jnp.dot(hid, wd_ref[0], preferred_element_type=F32)

    @pl.when(f == pl.num_programs(2) - 1)
    def _():
        n_e = pl.num_programs(0)
        lane = lax.broadcasted_iota(I32, (1, GATE_COLS), 1)
        own = (lane < 3 * n_e) & (lane % n_e == ei)
        g = jnp.sum(jnp.where(own, x_ref[:, d:].astype(F32), 0.0), axis=1, keepdims=True)
        o_ref[...] = (acc_ref[...] * g).astype(BF16)


def _combine_kernel(a_ref, s16_ref, npass_ref, x_ref, pos_ref, gfin_ref, o_hbm, out_ref, obuf, sem,
                    *, region, cap, final_norm):
    j = pl.program_id(0)
    nj = pl.num_programs(0)
    e = pos_ref.shape[0]
    slot = j % 2

    def window_start(i, jj, p):
        return jnp.minimum(s16_ref[i * nj + jj] + p * MOE_W, cap - MOE_W)

    def window_copy(i, src, buf):
        return pltpu.make_async_copy(o_hbm.at[pl.ds(src, MOE_W)], obuf.at[buf, pl.ds(i * MOE_W, MOE_W)], sem.at[buf])

    def fetch(jj, p, buf):
        for i in range(e):
            window_copy(i, pl.multiple_of(i * region + window_start(i, jj, p), SUB), buf).start()

    def wait(buf):
        for i in range(e):
            window_copy(i, 0, buf).wait()

    def one_hot(p):
        lo, hi, row0 = [], [], []
        for i in range(e):
            a = a_ref[i * nj + j]
            first = s16_ref[i * nj + j] + p * MOE_W - a
            lo.append(first)
            hi.append(first + MOE_W)
            row0.append(a - window_start(i, j, p))
        return _one_hot_slots(pos_ref, row0, lo, hi)

    @pl.when(j == 0)
    def _():
        fetch(0, 0, 0)

    @pl.when(j + 1 < nj)
    def _():
        fetch(j + 1, 0, 1 - slot)

    onehot = one_hot(0)
    wait(slot)
    y = x_ref[...] + lax.dot_general(onehot, obuf[slot], (((0,), (0,)), ((), ())), preferred_element_type=F32)

    def extra_pass(p, acc):
        fetch(j, p, 2)
        oh = one_hot(p)
        wait(2)
        return acc + lax.dot_general(oh, obuf[2], (((0,), (0,)), ((), ())), preferred_element_type=F32)

    y = lax.fori_loop(1, npass_ref[j], extra_pass, y)
    out_ref[...] = _rms(y, gfin_ref[...]) if final_norm else y


def _largest_tile(n, options):
    for t in options:
        if n % t == 0:
            return t
    raise ValueError(f"no tile in {options} divides {n}")


def expert_choice_moe(x, h, probs_t, w_gate, w_up, w_down, g_final=None):
    n, d = x.shape
    e = probs_t.shape[0]
    f = w_gate.shape[2]
    cap = EC_CAPACITY * n // e
    nj = n // MOE_T
    da = d + GATE_COLS
    assert 3 * e <= GATE_COLS and cap % SUB == 0 and cap >= MOE_W
    gates, pos, cnt = expert_choice_select(probs_t)

    ends = jnp.cumsum(cnt, axis=1)
    a = ends - cnt
    off = a % SUB
    s16 = a - off
    tot = off + cnt
    group = tot // SUB * SUB
    cpass = jnp.where(tot % SUB != 0, group // MOE_W, -1)
    coff = group % MOE_W
    npass = jnp.maximum(1, jnp.max((tot + MOE_W - 1) // MOE_W, axis=0))
    flat = lambda v: v.astype(I32).reshape(-1)

    tm = _largest_tile(cap, (1024, 512, 256, 128))
    zr = 256
    region = cap + -(-(MOE_T + 2 * MOE_W) // max(tm, zr)) * max(tm, zr)
    mt = cap // tm

    xs = pl.pallas_call(
        functools.partial(_gather_kernel, region=region, cap=cap),
        grid_spec=pltpu.PrefetchScalarGridSpec(
            num_scalar_prefetch=5,
            grid=(nj,),
            in_specs=[pl.BlockSpec((MOE_T, d), lambda i, *_: (i, 0)),
                      pl.BlockSpec((e, MOE_T), lambda i, *_: (0, i)),
                      pl.BlockSpec((e, MOE_T), lambda i, *_: (0, i))],
            out_specs=pl.BlockSpec(memory_space=pl.ANY),
            scratch_shapes=[pltpu.VMEM((2, e * MOE_W, da), BF16),
                            pltpu.VMEM((e * SUB, da), BF16),
                            pltpu.VMEM((zr, da), BF16),
                            pltpu.SMEM((1,), I32),
                            pltpu.SemaphoreType.DMA((2,))]),
        out_shape=jax.ShapeDtypeStruct((e * region, da), BF16),
        compiler_params=_cparams(("arbitrary",)),
    )(flat(s16), flat(off), flat(cpass), flat(coff), npass.astype(I32), h, gates, pos)

    fc = f // 2 if (f > 1024 and (f // 2) % 128 == 0) else f
    ft = f // fc
    rt = region // tm
    out_slots = pl.pallas_call(
        _ffn_kernel,
        grid=(e, mt, ft),
        in_specs=[pl.BlockSpec((tm, da), lambda ei, m, fi: (ei * rt + m, 0)),
                  pl.BlockSpec((1, d, fc), lambda ei, m, fi: (ei, 0, fi)),
                  pl.BlockSpec((1, d, fc), lambda ei, m, fi: (ei, 0, fi)),
                  pl.BlockSpec((1, fc, d), lambda ei, m, fi: (ei, fi, 0))],
        out_specs=pl.BlockSpec((tm, d), lambda ei, m, fi: (ei * mt + m, 0)),
        out_shape=jax.ShapeDtypeStruct((e * cap, d), BF16),
        scratch_shapes=[pltpu.VMEM((tm, d), F32)],
        compiler_params=_cparams(("parallel", "parallel", "arbitrary")),
    )(xs, w_gate, w_up, w_down)

    final_norm = g_final is not None
    gfin = (g_final if final_norm else jnp.ones((d,), F32)).reshape(1, d)
    return pl.pallas_call(
        functools.partial(_combine_kernel, region=cap, cap=cap, final_norm=final_norm),
        grid_spec=pltpu.PrefetchScalarGridSpec(
            num_scalar_prefetch=3,
            grid=(nj,),
            in_specs=[pl.BlockSpec((MOE_T, d), lambda i, *_: (i, 0)),
                      pl.BlockSpec((e, MOE_T), lambda i, *_: (0, i)),
                      pl.BlockSpec((1, d), lambda i, *_: (0, 0)),
                      pl.BlockSpec(memory_space=pl.ANY)],
            out_specs=pl.BlockSpec((MOE_T, d), lambda i, *_: (i, 0)),
            scratch_shapes=[pltpu.VMEM((3, e * MOE_W, d), BF16),
                            pltpu.SemaphoreType.DMA((3,))]),
        out_shape=jax.ShapeDtypeStruct((n, d), F32),
        compiler_params=_cparams(("arbitrary",)),
    )(flat(a), flat(s16), npass.astype(I32), x, pos, gfin, out_slots)


def _trunk(x3, params):
    batch, seq, d = x3.shape
    x = x3.reshape(batch * seq, d)
    depth = params["norm_mix"].shape[0]
    for i in range(depth):
        jm = i // 2
        if i % 2 == 0:
            a = fourier_core(x, params["norm_mix"][i], params["w_fourier_in"][jm], batch, seq)
            w_o = params["w_fourier_out"][jm]
        else:
            qkv = norm_matmul(x, params["norm_mix"][i], params["w_qkv"][jm])
            a = attention_core(qkv, params["attn_tables"][jm], batch, seq)
            w_o = params["w_attn_out"][jm]
        x, h, probs_t = post_mixer(a, x, w_o, params["norm_ffn"][i], params["w_router2"][i], params["n_experts"])
        g_final = params["norm_final"] if i == depth - 1 else None
        x = expert_choice_moe(x, h, probs_t, params["w_gate"][i], params["w_up"][i], params["w_down"][i], g_final)
    return x.reshape(batch, seq, d)


def kernel(x_prompt, x_sample, norm_mix, norm_ffn, norm_final, w_fourier_in, w_fourier_out, w_qkv, w_attn_out,
           rel_pos_bias, w_router, w_gate, w_up, w_down):
    params = dict(
        norm_mix=norm_mix, norm_ffn=norm_ffn, norm_final=norm_final,
        attn_tables=[attention_tables(rel_pos_bias[i]) for i in range(rel_pos_bias.shape[0])],
        w_fourier_in=w_fourier_in.astype(BF16), w_fourier_out=w_fourier_out.astype(BF16),
        w_qkv=w_qkv.astype(BF16), w_attn_out=w_attn_out.astype(BF16),
        w_router2=[split_router_weights(w_router[i]) for i in range(w_router.shape[0])],
        n_experts=w_router.shape[2],
        w_gate=w_gate.astype(BF16), w_up=w_up.astype(BF16), w_down=w_down.astype(BF16))
    return (_trunk(x_prompt, params), _trunk(x_sample, params))
```

```python
import functools
import math

import numpy as np
import jax
import jax.numpy as jnp
from jax import lax
from jax.experimental import pallas as pl
from jax.experimental.pallas import tpu as pltpu

EPS = 1e-6
GRID_W = 64
N_HEADS = 16
HEAD_DIM = 64
NA_KH = 8
NA_KW = 16
F_GROUPS = 4
EC_CAPACITY = 2
NEG_INF = -1e30

ROW_TILE = 512
ATT_ROWS = 4
MOE_T = 256
MOE_W = 64
SUB = 16
LANES = 128
GATE_COLS = 128
VMEM_LIMIT = 56 * 1024 * 1024

F32 = jnp.float32
BF16 = jnp.bfloat16
I32 = jnp.int32


def _cparams(sem):
    return pltpu.CompilerParams(dimension_semantics=sem, vmem_limit_bytes=VMEM_LIMIT)


def _rms(x, g):
    r = lax.rsqrt(jnp.mean(x * x, axis=-1, keepdims=True) + EPS)
    return (x * r) * g


def _norm_matmul_kernel(x_ref, g_ref, w_ref, o_ref, *, col_chunk):
    h = _rms(x_ref[...], g_ref[...]).astype(BF16)
    for c in range(o_ref.shape[1] // col_chunk):
        sl = slice(c * col_chunk, (c + 1) * col_chunk)
        o_ref[:, sl] = jnp.dot(h, w_ref[:, sl], preferred_element_type=F32).astype(o_ref.dtype)


def norm_matmul(x, g, w):
    n, d = x.shape
    m = w.shape[1]
    return pl.pallas_call(
        functools.partial(_norm_matmul_kernel, col_chunk=min(m, 1024)),
        grid=(n // ROW_TILE,),
        in_specs=[pl.BlockSpec((ROW_TILE, d), lambda i: (i, 0)),
                  pl.BlockSpec((1, d), lambda i: (0, 0)),
                  pl.BlockSpec((d, m), lambda i: (0, 0))],
        out_specs=pl.BlockSpec((ROW_TILE, m), lambda i: (i, 0)),
        out_shape=jax.ShapeDtypeStruct((n, m), BF16),
        compiler_params=_cparams(("parallel",)),
    )(x, g.reshape(1, d), w)


def _fourier_in_kernel(x_ref, g_ref, w_ref, cs_ref, vr_ref, vi_ref):
    h = _rms(x_ref[...], g_ref[...]).astype(BF16)
    u = jnp.dot(h, w_ref[...], preferred_element_type=F32).astype(BF16)
    gd = cs_ref.shape[0]
    for gi in range(u.shape[1] // gd):
        sl = slice(gi * gd, (gi + 1) * gd)
        ab = jnp.dot(u[:, sl], cs_ref[...], preferred_element_type=F32)
        vr_ref[:, sl] = ab[:, :gd].astype(BF16)
        vi_ref[:, sl] = ab[:, gd:].astype(BF16)


def _dft_a_kernel(vr_ref, vi_ref, m_ref, yr_ref, yi_ref, in_r, in_i, out_r, out_i):
    n1, t2, dc = vr_ref.shape
    chunks = [slice(c * LANES, (c + 1) * LANES) for c in range(dc // LANES)]
    for c, sl in enumerate(chunks):
        in_r[c] = vr_ref[:, :, sl].astype(F32).reshape(n1 * t2, LANES)
        in_i[c] = vi_ref[:, :, sl].astype(F32).reshape(n1 * t2, LANES)
    for s in range(t2):
        rows = pl.ds(s, n1, stride=t2)
        v = jnp.concatenate([jnp.concatenate([ref[c, rows, :] for c in range(len(chunks))], axis=1)
                             for ref in (in_r, in_i)], axis=0).astype(BF16)
        y = jnp.dot(m_ref[...], v, preferred_element_type=F32)
        for c, sl in enumerate(chunks):
            out_r[c, rows, :] = y[:n1, sl]
            out_i[c, rows, :] = y[n1:, sl]
    for c, sl in enumerate(chunks):
        yr_ref[:, :, sl] = out_r[c].reshape(n1, t2, LANES).astype(BF16)
        yi_ref[:, :, sl] = out_i[c].reshape(n1, t2, LANES).astype(BF16)


def _dft_c_kernel(yr_ref, yi_ref, g_ref, z_ref, out_z):
    kb, n2, d = yr_ref.shape
    chunks = [slice(c * LANES, (c + 1) * LANES) for c in range(d // LANES)]
    for i in range(kb):
        ycat = jnp.concatenate([yr_ref[i], yi_ref[i]], axis=0)
        z = jnp.dot(g_ref[i], ycat, preferred_element_type=F32)
        for c, sl in enumerate(chunks):
            out_z[c, pl.ds(i, n2, stride=kb), :] = z[:, sl]
    for c, sl in enumerate(chunks):
        z_ref[:, :, sl] = out_z[c].reshape(n2, kb, LANES).astype(BF16)


def _dft_tables(s, n1, n2, gd):
    def cs(num, den):
        ang = (2.0 * math.pi / den) * (num % den).astype(F32)
        return jnp.cos(ang), jnp.sin(ang)

    ic = jnp.arange(gd, dtype=I32)
    cc, sc = cs(ic[:, None] * ic[None, :], gd)
    chan = jnp.concatenate([cc, -sc], axis=1).astype(BF16)
    i1 = jnp.arange(n1, dtype=I32)
    c1, s1 = cs(i1[:, None] * i1[None, :], n1)
    m1 = jnp.concatenate([jnp.concatenate([c1, s1], axis=1),
                          jnp.concatenate([-s1, c1], axis=1)], axis=0).astype(BF16)
    k1 = jnp.arange(n1, dtype=I32)[:, None, None]
    k2 = jnp.arange(n2, dtype=I32)[None, :, None]
    s2 = jnp.arange(n2, dtype=I32)[None, None, :]
    cg, sg = cs(((n1 * k2 + k1) % s) * s2, s)
    scale = 1.0 / math.sqrt(float(s) * float(gd))
    g = (jnp.concatenate([cg, sg], axis=2) * scale).astype(BF16)
    return chan, m1, g


def fourier_core(x, g_norm, w_in, batch, seq):
    n, d = x.shape
    gd = d // F_GROUPS
    n1 = 1 << ((seq.bit_length() - 1 + 1) // 2)
    n2 = seq // n1
    assert n1 * n2 == seq and n1 % 16 == 0 and n2 % 16 == 0
    chan, m1, gtab = _dft_tables(seq, n1, n2, gd)

    vr, vi = pl.pallas_call(
        _fourier_in_kernel,
        grid=(n // ROW_TILE,),
        in_specs=[pl.BlockSpec((ROW_TILE, d), lambda i: (i, 0)),
                  pl.BlockSpec((1, d), lambda i: (0, 0)),
                  pl.BlockSpec((d, d), lambda i: (0, 0)),
                  pl.BlockSpec((gd, 2 * gd), lambda i: (0, 0))],
        out_specs=[pl.BlockSpec((ROW_TILE, d), lambda i: (i, 0))] * 2,
        out_shape=[jax.ShapeDtypeStruct((n, d), BF16)] * 2,
        compiler_params=_cparams(("parallel",)),
    )(x, g_norm.reshape(1, d), w_in, chan)

    t2, dc = SUB, min(d, 512)
    a_spec = pl.BlockSpec((n1, t2, dc), lambda b, j, c: (b, j, c))
    yr, yi = pl.pallas_call(
        _dft_a_kernel,
        grid=(batch, n2 // t2, d // dc),
        in_specs=[a_spec, a_spec, pl.BlockSpec((2 * n1, 2 * n1), lambda b, j, c: (0, 0))],
        out_specs=[a_spec, a_spec],
        out_shape=[jax.ShapeDtypeStruct((batch * n1, n2, d), BF16)] * 2,
        scratch_shapes=[pltpu.VMEM((dc // LANES, n1 * t2, LANES), F32) for _ in range(4)],
        compiler_params=_cparams(("parallel", "parallel", "parallel")),
    )(vr.reshape(batch * n1, n2, d), vi.reshape(batch * n1, n2, d), m1)

    kb = SUB
    c_spec = pl.BlockSpec((kb, n2, d), lambda b, k: (b * (n1 // kb) + k, 0, 0))
    z = pl.pallas_call(
        _dft_c_kernel,
        grid=(batch, n1 // kb),
        in_specs=[c_spec, c_spec, pl.BlockSpec((kb, n2, 2 * n2), lambda b, k: (k, 0, 0))],
        out_specs=pl.BlockSpec((n2, kb, d), lambda b, k: (b, k, 0)),
        out_shape=jax.ShapeDtypeStruct((batch * n2, n1, d), BF16),
        scratch_shapes=[pltpu.VMEM((d // LANES, n2 * kb, LANES), F32)],
        compiler_params=_cparams(("parallel", "parallel")),
    )(yr, yi, gtab)
    return z.reshape(n, d)


def _attn_kernel(q_ref, kp_ref, kc_ref, kn_ref, vp_ref, vc_ref, vn_ref, bias_ref, qf_ref, kf_ref, o_ref):
    q = q_ref[...] * jnp.asarray(HEAD_DIM ** -0.5, BF16)
    k = jnp.concatenate([kp_ref[...], kc_ref[...], kn_ref[...]], axis=0)
    v = jnp.concatenate([vp_ref[...], vc_ref[...], vn_ref[...]], axis=0)
    lane = lax.broadcasted_iota(I32, (1, 2 * HEAD_DIM), 1)
    for hp in range(N_HEADS // 2):
        sl = slice(hp * 2 * HEAD_DIM, (hp + 1) * 2 * HEAD_DIM)
        q2, k2, v2 = q[:, sl], k[:, sl], v[:, sl]
        out = None
        for s in range(2):
            in_head = (lane >= s * HEAD_DIM) & (lane < (s + 1) * HEAD_DIM)
            qm = jnp.where(in_head, q2, qf_ref[s])
            km = jnp.where(in_head, k2, kf_ref[0, s])
            vm = jnp.where(in_head, v2, jnp.ones_like(v2))
            sc = lax.dot_general(qm, km, (((1,), (1,)), ((), ())), preferred_element_type=F32)
            sc = sc + bias_ref[2 * hp + s]
            m = jnp.max(sc, axis=1, keepdims=True)
            p = jnp.exp(sc - m).astype(BF16)
            pv = jnp.dot(p, vm, preferred_element_type=F32)
            row_sum = pltpu.roll(pv, HEAD_DIM, axis=1)
            pv = jnp.where(in_head, pv / row_sum, 0.0)
            out = pv if out is None else out + pv
        o_ref[:, sl] = out.astype(BF16)


def attention_tables(rpb):
    h = rpb.shape[0]
    wk = 3 * ATT_ROWS
    idx = np.clip(np.arange(2 * GRID_W - 1) - (GRID_W - NA_KW), 0, 2 * NA_KW - 2)
    v = jnp.concatenate([rpb[:, :, idx], jnp.zeros(rpb.shape[:2] + (1,), rpb.dtype)], axis=-1)
    m = jnp.tile(v, (1, 1, GRID_W))[:, :, :GRID_W * (2 * GRID_W - 1)]
    blk = m.reshape(h, 2 * NA_KH - 1, GRID_W, 2 * GRID_W - 1)[:, :, :, GRID_W - 1:]
    lo = NA_KH - 1 - ATT_ROWS
    per_row = jnp.stack([blk[:, lo - iq: lo - iq + wk] for iq in range(ATT_ROWS)], axis=1)
    bias = jnp.transpose(per_row, (0, 1, 3, 2, 4)).reshape(h, ATT_ROWS * GRID_W, wk * GRID_W).astype(F32)

    n_ws = GRID_W - NA_KW + 1
    assert ATT_ROWS + n_ws <= HEAD_DIM
    iq = np.arange(ATT_ROWS)[:, None]
    kr = np.arange(wk)[None, :]
    row_ok = [
        (kr >= ATT_ROWS) & (kr < ATT_ROWS + NA_KH) & (iq >= 0),
        (kr >= iq) & (kr < iq + NA_KH),
        (kr < NA_KH) & (iq >= 0),
    ]
    qf = np.zeros((ATT_ROWS, GRID_W, HEAD_DIM), np.float32)
    qf[np.arange(ATT_ROWS), :, np.arange(ATT_ROWS)] = 1.0
    ws = np.clip(np.arange(GRID_W) - NA_KW // 2, 0, GRID_W - NA_KW)
    qf[:, np.arange(GRID_W), ATT_ROWS + ws] = 1.0
    kcol = np.arange(GRID_W)[:, None]
    w = np.arange(n_ws)[None, :]
    col_bad = np.where((kcol >= w) & (kcol < w + NA_KW), 0.0, NEG_INF)
    kf = np.zeros((3, wk, GRID_W, HEAD_DIM), np.float32)
    for case, ok in enumerate(row_ok):
        kf[case, :, :, :ATT_ROWS] = np.where(np.broadcast_to(ok, (ATT_ROWS, wk)).T, 0.0, NEG_INF)[:, None, :]
        kf[case, :, :, ATT_ROWS:ATT_ROWS + n_ws] = col_bad[None]
    qf = qf.reshape(ATT_ROWS * GRID_W, HEAD_DIM)
    kf = kf.reshape(3, wk * GRID_W, HEAD_DIM)
    zq, zk = np.zeros_like(qf), np.zeros_like(kf)
    qfeat = np.stack([np.concatenate([zq, qf], axis=-1), np.concatenate([qf, zq], axis=-1)])
    kfeat = np.stack([np.concatenate([zk, kf], axis=-1), np.concatenate([kf, zk], axis=-1)], axis=1)
    return bias, jnp.asarray(qfeat, BF16), jnp.asarray(kfeat, BF16)


def attention_core(qkv, tables, batch, seq):
    n = qkv.shape[0]
    d = N_HEADS * HEAD_DIM
    rows = seq // GRID_W
    groups = rows // ATT_ROWS
    assert rows % ATT_ROWS == 0 and rows >= 4 * ATT_ROWS and NA_KH == 2 * ATT_ROWS
    rq = ATT_ROWS * GRID_W
    bias, qfeat, kfeat = tables

    def blk(off, col):
        def index_map(b, g):
            return (b * groups + jnp.clip(g + off, 0, groups - 1), col)
        return pl.BlockSpec((rq, d), index_map)

    def case_map(b, g):
        return (jnp.where(g == 0, 0, jnp.where(g == groups - 1, 2, 1)), 0, 0, 0)

    return pl.pallas_call(
        _attn_kernel,
        grid=(batch, groups),
        in_specs=[blk(0, 0), blk(-1, 1), blk(0, 1), blk(1, 1), blk(-1, 2), blk(0, 2), blk(1, 2),
                  pl.BlockSpec((N_HEADS, rq, 3 * rq), lambda b, g: (0, 0, 0)),
                  pl.BlockSpec((2, rq, 2 * HEAD_DIM), lambda b, g: (0, 0, 0)),
                  pl.BlockSpec((1, 2, 3 * rq, 2 * HEAD_DIM), case_map)],
        out_specs=pl.BlockSpec((rq, d), lambda b, g: (b * groups + g, 0)),
        out_shape=jax.ShapeDtypeStruct((n, d), BF16),
        compiler_params=_cparams(("parallel", "parallel")),
    )(qkv, qkv, qkv, qkv, qkv, qkv, qkv, bias, qfeat, kfeat)


def _post_kernel(a_ref, x_ref, wo_ref, g_ref, wr_ref, xo_ref, h_ref, p_ref):
    for r in range(x_ref.shape[0] // LANES):
        rows = slice(r * LANES, (r + 1) * LANES)
        xn = x_ref[rows, :] + jnp.dot(a_ref[rows, :], wo_ref[...], preferred_element_type=F32)
        xo_ref[rows, :] = xn
        hf = _rms(xn, g_ref[...])
        h_hi = hf.astype(BF16)
        h_ref[rows, :] = h_hi
        lg = jnp.dot(h_hi, wr_ref[...], preferred_element_type=F32)
        logits = lg.T[:p_ref.shape[0]]
        m = jnp.max(logits, axis=0, keepdims=True)
        e = jnp.exp(logits - m)
        p_ref[:, rows] = e / jnp.sum(e, axis=0, keepdims=True)


def pad_router_weights(w_router):
    return jnp.pad(w_router.astype(BF16), ((0, 0), (0, LANES - w_router.shape[1])))


def post_mixer(a, x, w_o, g_ffn, w_router2, e):
    n, d = x.shape
    return pl.pallas_call(
        _post_kernel,
        grid=(n // ROW_TILE,),
        in_specs=[pl.BlockSpec((ROW_TILE, d), lambda i: (i, 0)),
                  pl.BlockSpec((ROW_TILE, d), lambda i: (i, 0)),
                  pl.BlockSpec((d, d), lambda i: (0, 0)),
                  pl.BlockSpec((1, d), lambda i: (0, 0)),
                  pl.BlockSpec((d, LANES), lambda i: (0, 0))],
        out_specs=[pl.BlockSpec((ROW_TILE, d), lambda i: (i, 0)),
                   pl.BlockSpec((ROW_TILE, d), lambda i: (i, 0)),
                   pl.BlockSpec((e, ROW_TILE), lambda i: (0, i))],
        out_shape=[jax.ShapeDtypeStruct((n, d), F32),
                   jax.ShapeDtypeStruct((n, d), BF16),
                   jax.ShapeDtypeStruct((e, n), F32)],
        compiler_params=_cparams(("parallel",)),
    )(a, x, w_o, g_ffn.reshape(1, d), w_router2)


def _threshold_kernel(p_ref, thr_ref, need_ref, *, cap):
    e = p_ref.shape[0]

    def count_ge(cand):
        bits = pltpu.bitcast(p_ref[...], I32)
        return jnp.sum((bits >= cand).astype(F32), axis=1, keepdims=True)

    def body(i, prefix):
        cand = prefix | jnp.left_shift(jnp.int32(1), 30 - i)
        return jnp.where(count_ge(cand) >= cap, cand, prefix)

    thr = lax.fori_loop(0, 31, body, jnp.zeros((e, 1), I32))
    n_gt = count_ge(thr + 1)
    thr_ref[...] = jnp.broadcast_to(thr, thr_ref.shape)
    need_ref[...] = jnp.broadcast_to(cap - n_gt, need_ref.shape)


def _select_kernel(p_ref, thr_ref, need_ref, ut_ref, gate_ref, pos_ref, cnt_ref, carry_ref):
    @pl.when(pl.program_id(0) == 0)
    def _():
        carry_ref[...] = jnp.zeros_like(carry_ref)

    thr = thr_ref[:, :1]
    ut = ut_ref[...]
    t = ut.shape[0]
    for i in range(cnt_ref.shape[0]):
        cols = slice(i * t, (i + 1) * t)
        p = p_ref[:, cols]
        bits = pltpu.bitcast(p, I32)
        gt = bits > thr
        eq = bits == thr
        eq_f = jnp.where(eq, 1.0, 0.0)
        eq_before = jnp.dot(eq_f.astype(BF16), ut, preferred_element_type=F32) + carry_ref[:, :1]
        sel = gt | (eq & (eq_before < need_ref[:, :1]))
        carry_ref[...] = carry_ref[...] + jnp.sum(eq_f, axis=1, keepdims=True)
        sel_f = jnp.where(sel, 1.0, 0.0)
        rank = jnp.dot(sel_f.astype(BF16), ut, preferred_element_type=F32)
        pos_ref[:, cols] = jnp.where(sel, rank.astype(I32), -1)
        gate_ref[:, cols] = jnp.where(sel, p, 0.0)
        cnt = jnp.sum(sel_f, axis=1, keepdims=True).astype(I32)
        cnt_ref[i] = jnp.broadcast_to(cnt, cnt_ref.shape[1:])


def expert_choice_select(probs_t):
    e, n = probs_t.shape
    cap = EC_CAPACITY * n // e
    j = n // MOE_T
    thr, need = pl.pallas_call(
        functools.partial(_threshold_kernel, cap=cap),
        out_shape=[jax.ShapeDtypeStruct((e, 128), I32), jax.ShapeDtypeStruct((e, 128), F32)],
        compiler_params=pltpu.CompilerParams(vmem_limit_bytes=VMEM_LIMIT),
    )(probs_t)
    it = np.arange(MOE_T)
    ut = jnp.asarray((it[:, None] < it[None, :]).astype(np.float32), BF16)
    tiles = _largest_tile(j, (8, 4, 2, 1))
    gates, pos, cnt = pl.pallas_call(
        _select_kernel,
        grid=(j // tiles,),
        in_specs=[pl.BlockSpec((e, tiles * MOE_T), lambda i: (0, i)),
                  pl.BlockSpec((e, 128), lambda i: (0, 0)),
                  pl.BlockSpec((e, 128), lambda i: (0, 0)),
                  pl.BlockSpec((MOE_T, MOE_T), lambda i: (0, 0))],
        out_specs=[pl.BlockSpec((e, tiles * MOE_T), lambda i: (0, i)),
                   pl.BlockSpec((e, tiles * MOE_T), lambda i: (0, i)),
                   pl.BlockSpec((tiles, e, 128), lambda i: (i, 0, 0))],
        out_shape=[jax.ShapeDtypeStruct((e, n), F32),
                   jax.ShapeDtypeStruct((e, n), I32),
                   jax.ShapeDtypeStruct((j, e, 128), I32)],
        scratch_shapes=[pltpu.VMEM((e, 128), F32)],
        compiler_params=_cparams(("arbitrary",)),
    )(probs_t, thr, need, ut)
    return gates, pos, cnt[:, :, 0].T


def _one_hot_slots(pos_ref, row_of_rank0, lo, hi):
    e, t = pos_ref.shape
    rows = lax.broadcasted_iota(I32, (MOE_W, t), 0)
    blocks = []
    for i in range(e):
        r = pos_ref[i:i + 1, :]
        tgt = jnp.where((r >= jnp.maximum(lo[i], 0)) & (r < hi[i]), r + row_of_rank0[i], -1)
        blocks.append(jnp.where(rows == tgt, 1.0, 0.0).astype(BF16))
    return jnp.concatenate(blocks, axis=0)


def _gather_kernel(s16_ref, off_ref, cpass_ref, coff_ref, npass_ref, h_ref, gate_ref, pos_ref, xs_ref,
                   ybuf, carry, zbuf, nbatch, sem, *, region, cap):
    j = pl.program_id(0)
    nj = pl.num_programs(0)
    e, t = pos_ref.shape
    d = h_ref.shape[1]

    @pl.when(j == 0)
    def _():
        nbatch[0] = 0
        carry[...] = jnp.zeros_like(carry)

    g = gate_ref[...]
    g_hi = g.astype(BF16)
    r1 = g - g_hi.astype(F32)
    g_mid = r1.astype(BF16)
    g_lo = (r1 - g_mid.astype(F32)).astype(BF16)
    g3 = jnp.concatenate([g_hi, g_mid, g_lo, jnp.zeros((GATE_COLS - 3 * e, t), BF16)], axis=0)

    def window_copy(slot, i, dst):
        return pltpu.make_async_copy(ybuf.at[slot, pl.ds(i * MOE_W, MOE_W)], xs_ref.at[pl.ds(dst, MOE_W)], sem.at[0])

    def wait_batch(slot):
        for i in range(e):
            window_copy(slot, i, 0).wait()

    def one_pass(p, carry_in):
        k = nbatch[0]
        slot = k % 2
        off = [off_ref[i * nj + j] for i in range(e)]
        shift = [off[i] - p * MOE_W for i in range(e)]
        onehot = _one_hot_slots(pos_ref, shift, [-shift[i] for i in range(e)], [MOE_W - shift[i] for i in range(e)])
        ybuf[slot, :, :d] = jnp.dot(onehot, h_ref[...], preferred_element_type=F32).astype(BF16)
        ybuf[slot, :, d:] = lax.dot_general(onehot, g3, (((1,), (1,)), ((), ())),
                                            preferred_element_type=F32).astype(BF16)
        for i in range(e):
            @pl.when((p == 0) & (off[i] > 0))
            def _():
                rows = pl.ds(i * MOE_W, SUB)
                ybuf[slot, rows, :] = (ybuf[slot, rows, :].astype(F32)
                                       + carry[pl.ds(i * SUB, SUB), :].astype(F32)).astype(BF16)

            @pl.when(p == cpass_ref[i * nj + j])
            def _():
                src = pl.multiple_of(i * MOE_W + coff_ref[i * nj + j], SUB)
                carry[pl.ds(i * SUB, SUB), :] = ybuf[slot, pl.ds(src, SUB), :]

        @pl.when(k > 0)
        def _():
            wait_batch(1 - slot)

        for i in range(e):
            dst = pl.multiple_of(i * region + s16_ref[i * nj + j] + p * MOE_W, SUB)
            window_copy(slot, i, dst).start()
        nbatch[0] = k + 1
        return carry_in

    lax.fori_loop(0, npass_ref[j], one_pass, 0)

    @pl.when(j == nj - 1)
    def _():
        wait_batch((nbatch[0] - 1) % 2)
        zbuf[...] = jnp.zeros_like(zbuf)
        zr = zbuf.shape[0]
        cps = [pltpu.make_async_copy(zbuf, xs_ref.at[pl.ds(i * region + cap + c * zr, zr)], sem.at[1])
               for i in range(e) for c in range((region - cap) // zr)]
        for cp in cps:
            cp.start()
        for cp in cps:
            cp.wait()


def _ffn_kernel(x_ref, wg_ref, wu_ref, wd_ref, o_ref, acc_ref):
    ei, f = pl.program_id(0), pl.program_id(2)
    d = o_ref.shape[1]

    @pl.when(f == 0)
    def _():
        acc_ref[...] = jnp.zeros_like(acc_ref)

    x = x_ref[:, :d]
    gate = jnp.dot(x, wg_ref[0], preferred_element_type=F32)
    up = jnp.dot(x, wu_ref[0], preferred_element_type=F32)
    hid = (gate * jax.nn.sigmoid(gate) * up).astype(BF16)
    acc_ref[...] += jnp.dot(hid, wd_ref[0], preferred_element_type=F32)

    @pl.when(f == pl.num_programs(2) - 1)
    def _():
        n_e = pl.num_programs(0)
        lane = lax.broadcasted_iota(I32, (1, GATE_COLS), 1)
        own = (lane < 3 * n_e) & (lane % n_e == ei)
        g = jnp.sum(jnp.where(own, x_ref[:, d:].astype(F32), 0.0), axis=1, keepdims=True)
        o_ref[...] = (acc_ref[...] * g).astype(BF16)


def _combine_kernel(a_ref, s16_ref, npass_ref, x_ref, pos_ref, gfin_ref, o_hbm, out_ref, obuf, sem,
                    *, region, cap, final_norm):
    j = pl.program_id(0)
    nj = pl.num_programs(0)
    e = pos_ref.shape[0]
    slot = j % 2

    def window_start(i, jj, p):
        return jnp.minimum(s16_ref[i * nj + jj] + p * MOE_W, cap - MOE_W)

    def window_copy(i, src, buf):
        return pltpu.make_async_copy(o_hbm.at[pl.ds(src, MOE_W)], obuf.at[buf, pl.ds(i * MOE_W, MOE_W)], sem.at[buf])

    def fetch(jj, p, buf):
        for i in range(e):
            window_copy(i, pl.multiple_of(i * region + window_start(i, jj, p), SUB), buf).start()

    def wait(buf):
        for i in range(e):
            window_copy(i, 0, buf).wait()

    def one_hot(p):
        lo, hi, row0 = [], [], []
        for i in range(e):
            a = a_ref[i * nj + j]
            first = s16_ref[i * nj + j] + p * MOE_W - a
            lo.append(first)
            hi.append(first + MOE_W)
            row0.append(a - window_start(i, j, p))
        return _one_hot_slots(pos_ref, row0, lo, hi)

    @pl.when(j == 0)
    def _():
        fetch(0, 0, 0)

    @pl.when(j + 1 < nj)
    def _():
        fetch(j + 1, 0, 1 - slot)

    onehot = one_hot(0)
    wait(slot)
    y = x_ref[...] + lax.dot_general(onehot, obuf[slot], (((0,), (0,)), ((), ())), preferred_element_type=F32)

    def extra_pass(p, acc):
        fetch(j, p, 2)
        oh = one_hot(p)
        wait(2)
        return acc + lax.dot_general(oh, obuf[2], (((0,), (0,)), ((), ())), preferred_element_type=F32)

    y = lax.fori_loop(1, npass_ref[j], extra_pass, y)
    out_ref[...] = _rms(y, gfin_ref[...]) if final_norm else y


def _largest_tile(n, options):
    for t in options:
        if n % t == 0:
            return t
    raise ValueError(f"no tile in {options} divides {n}")


def expert_choice_moe(x, h, probs_t, w_gate, w_up, w_down, layer, g_final=None):
    n, d = x.shape
    e = probs_t.shape[0]
    f = w_gate.shape[3]
    cap = EC_CAPACITY * n // e
    nj = n // MOE_T
    da = d + GATE_COLS
    assert 3 * e <= GATE_COLS and cap % SUB == 0 and cap >= MOE_W
    gates, pos, cnt = expert_choice_select(probs_t)

    ends = jnp.cumsum(cnt, axis=1)
    a = ends - cnt
    off = a % SUB
    s16 = a - off
    tot = off + cnt
    group = tot // SUB * SUB
    cpass = jnp.where(tot % SUB != 0, group // MOE_W, -1)
    coff = group % MOE_W
    npass = jnp.maximum(1, jnp.max((tot + MOE_W - 1) // MOE_W, axis=0))
    flat = lambda v: v.astype(I32).reshape(-1)

    tm = _largest_tile(cap, (1024, 512, 256, 128))
    zr = 256
    region = cap + -(-(MOE_T + 2 * MOE_W) // max(tm, zr)) * max(tm, zr)
    mt = cap // tm

    xs = pl.pallas_call(
        functools.partial(_gather_kernel, region=region, cap=cap),
        grid_spec=pltpu.PrefetchScalarGridSpec(
            num_scalar_prefetch=5,
            grid=(nj,),
            in_specs=[pl.BlockSpec((MOE_T, d), lambda i, *_: (i, 0)),
                      pl.BlockSpec((e, MOE_T), lambda i, *_: (0, i)),
                      pl.BlockSpec((e, MOE_T), lambda i, *_: (0, i))],
            out_specs=pl.BlockSpec(memory_space=pl.ANY),
            scratch_shapes=[pltpu.VMEM((2, e * MOE_W, da), BF16),
                            pltpu.VMEM((e * SUB, da), BF16),
                            pltpu.VMEM((zr, da), BF16),
                            pltpu.SMEM((1,), I32),
                            pltpu.SemaphoreType.DMA((2,))]),
        out_shape=jax.ShapeDtypeStruct((e * region, da), BF16),
        compiler_params=_cparams(("arbitrary",)),
    )(flat(s16), flat(off), flat(cpass), flat(coff), npass.astype(I32), h, gates, pos)

    fc = f // 2 if (f > 1024 and (f // 2) % 128 == 0) else f
    ft = f // fc
    rt = region // tm
    out_slots = pl.pallas_call(
        _ffn_kernel,
        grid=(e, mt, ft),
        in_specs=[pl.BlockSpec((tm, da), lambda ei, m, fi: (ei * rt + m, 0)),
                  pl.BlockSpec((None, 1, d, fc), lambda ei, m, fi: (layer, ei, 0, fi)),
                  pl.BlockSpec((None, 1, d, fc), lambda ei, m, fi: (layer, ei, 0, fi)),
                  pl.BlockSpec((None, 1, fc, d), lambda ei, m, fi: (layer, ei, fi, 0))],
        out_specs=pl.BlockSpec((tm, d), lambda ei, m, fi: (ei * mt + m, 0)),
        out_shape=jax.ShapeDtypeStruct((e * cap, d), BF16),
        scratch_shapes=[pltpu.VMEM((tm, d), F32)],
        compiler_params=_cparams(("parallel", "parallel", "arbitrary")),
    )(xs, w_gate, w_up, w_down)

    final_norm = g_final is not None
    gfin = (g_final if final_norm else jnp.ones((d,), F32)).reshape(1, d)
    return pl.pallas_call(
        functools.partial(_combine_kernel, region=cap, cap=cap, final_norm=final_norm),
        grid_spec=pltpu.PrefetchScalarGridSpec(
            num_scalar_prefetch=3,
            grid=(nj,),
            in_specs=[pl.BlockSpec((MOE_T, d), lambda i, *_: (i, 0)),
                      pl.BlockSpec((e, MOE_T), lambda i, *_: (0, i)),
                      pl.BlockSpec((1, d), lambda i, *_: (0, 0)),
                      pl.BlockSpec(memory_space=pl.ANY)],
            out_specs=pl.BlockSpec((MOE_T, d), lambda i, *_: (i, 0)),
            scratch_shapes=[pltpu.VMEM((3, e * MOE_W, d), BF16),
                            pltpu.SemaphoreType.DMA((3,))]),
        out_shape=jax.ShapeDtypeStruct((n, d), F32),
        compiler_params=_cparams(("arbitrary",)),
    )(flat(a), flat(s16), npass.astype(I32), x, pos, gfin, out_slots)


def _trunk(x3, params):
    batch, seq, d = x3.shape
    x = x3.reshape(batch * seq, d)
    depth = params["norm_mix"].shape[0]
    for i in range(depth):
        jm = i // 2
        if i % 2 == 0:
            a = fourier_core(x, params["norm_mix"][i], params["w_fourier_in"][jm], batch, seq)
            w_o = params["w_fourier_out"][jm]
        else:
            qkv = norm_matmul(x, params["norm_mix"][i], params["w_qkv"][jm])
            a = attention_core(qkv, params["attn_tables"][jm], batch, seq)
            w_o = params["w_attn_out"][jm]
        x, h, probs_t = post_mixer(a, x, w_o, params["norm_ffn"][i], params["w_router2"][i], params["n_experts"])
        g_final = params["norm_final"] if i == depth - 1 else None
        x = expert_choice_moe(x, h, probs_t, params["w_gate"], params["w_up"], params["w_down"], i, g_final)
    return x.reshape(batch, seq, d)


def kernel(x_prompt, x_sample, norm_mix, norm_ffn, norm_final, w_fourier_in, w_fourier_out, w_qkv, w_attn_out,
           rel_pos_bias, w_router, w_gate, w_up, w_down):
    params = dict(
        norm_mix=norm_mix, norm_ffn=norm_ffn, norm_final=norm_final,
        attn_tables=[attention_tables(rel_pos_bias[i]) for i in range(rel_pos_bias.shape[0])],
        w_fourier_in=w_fourier_in.astype(BF16), w_fourier_out=w_fourier_out.astype(BF16),
        w_qkv=w_qkv.astype(BF16), w_attn_out=w_attn_out.astype(BF16),
        w_router2=[pad_router_weights(w_router[i]) for i in range(w_router.shape[0])],
        n_experts=w_router.shape[2],
        w_gate=w_gate.astype(BF16), w_up=w_up.astype(BF16), w_down=w_down.astype(BF16))
    return (_trunk(x_prompt, params), _trunk(x_sample, params))
```

```python
import functools
import math

import numpy as np
import jax
import jax.numpy as jnp
from jax import lax
from jax.experimental import pallas as pl
from jax.experimental.pallas import tpu as pltpu

EPS = 1e-6
GRID_W = 64
N_HEADS = 16
HEAD_DIM = 64
NA_KH = 8
NA_KW = 16
F_GROUPS = 4
EC_CAPACITY = 2
NEG_INF = -1e30

ROW_TILE = 512
ATT_ROWS = 4
MOE_T = 256
MOE_W = 64
SUB = 16
LANES = 128
GATE_COLS = 128
VMEM_LIMIT = 56 * 1024 * 1024

F32 = jnp.float32
BF16 = jnp.bfloat16
I32 = jnp.int32


def _cparams(sem):
    return pltpu.CompilerParams(dimension_semantics=sem, vmem_limit_bytes=VMEM_LIMIT)


def _rms(x, g):
    r = lax.rsqrt(jnp.mean(x * x, axis=-1, keepdims=True) + EPS)
    return (x * r) * g


def _norm_matmul_kernel(x_ref, g_ref, w_ref, o_ref, *, col_chunk):
    h = _rms(x_ref[...], g_ref[...]).astype(BF16)
    for c in range(o_ref.shape[1] // col_chunk):
        sl = slice(c * col_chunk, (c + 1) * col_chunk)
        o_ref[:, sl] = jnp.dot(h, w_ref[:, sl], preferred_element_type=F32).astype(o_ref.dtype)


def norm_matmul(x, g, w):
    n, d = x.shape
    m = w.shape[1]
    return pl.pallas_call(
        functools.partial(_norm_matmul_kernel, col_chunk=min(m, 1024)),
        grid=(n // ROW_TILE,),
        in_specs=[pl.BlockSpec((ROW_TILE, d), lambda i: (i, 0)),
                  pl.BlockSpec((1, d), lambda i: (0, 0)),
                  pl.BlockSpec((d, m), lambda i: (0, 0))],
        out_specs=pl.BlockSpec((ROW_TILE, m), lambda i: (i, 0)),
        out_shape=jax.ShapeDtypeStruct((n, m), BF16),
        compiler_params=_cparams(("parallel",)),
    )(x, g.reshape(1, d), w)


def _fourier_in_kernel(x_ref, g_ref, w_ref, cs_ref, vr_ref, vi_ref):
    h = _rms(x_ref[...], g_ref[...]).astype(BF16)
    u = jnp.dot(h, w_ref[...], preferred_element_type=F32).astype(BF16)
    gd = cs_ref.shape[0]
    for gi in range(u.shape[1] // gd):
        sl = slice(gi * gd, (gi + 1) * gd)
        ab = jnp.dot(u[:, sl], cs_ref[...], preferred_element_type=F32)
        vr_ref[:, sl] = ab[:, :gd].astype(BF16)
        vi_ref[:, sl] = ab[:, gd:].astype(BF16)


def _dft_a_kernel(vr_ref, vi_ref, m_ref, yr_ref, yi_ref, in_r, in_i, out_r, out_i):
    n1, t2, dc = vr_ref.shape
    chunks = [slice(c * LANES, (c + 1) * LANES) for c in range(dc // LANES)]
    for c, sl in enumerate(chunks):
        in_r[c] = vr_ref[:, :, sl].astype(F32).reshape(n1 * t2, LANES)
        in_i[c] = vi_ref[:, :, sl].astype(F32).reshape(n1 * t2, LANES)
    for s in range(t2):
        rows = pl.ds(s, n1, stride=t2)
        v = jnp.concatenate([jnp.concatenate([ref[c, rows, :] for c in range(len(chunks))], axis=1)
                             for ref in (in_r, in_i)], axis=0).astype(BF16)
        y = jnp.dot(m_ref[...], v, preferred_element_type=F32)
        for c, sl in enumerate(chunks):
            out_r[c, rows, :] = y[:n1, sl]
            out_i[c, rows, :] = y[n1:, sl]
    for c, sl in enumerate(chunks):
        yr_ref[:, :, sl] = out_r[c].reshape(n1, t2, LANES).astype(BF16)
        yi_ref[:, :, sl] = out_i[c].reshape(n1, t2, LANES).astype(BF16)


def _dft_c_kernel(yr_ref, yi_ref, g_ref, z_ref, out_z):
    kb, n2, d = yr_ref.shape
    chunks = [slice(c * LANES, (c + 1) * LANES) for c in range(d // LANES)]
    for i in range(kb):
        ycat = jnp.concatenate([yr_ref[i], yi_ref[i]], axis=0)
        z = jnp.dot(g_ref[i], ycat, preferred_element_type=F32)
        for c, sl in enumerate(chunks):
            out_z[c, pl.ds(i, n2, stride=kb), :] = z[:, sl]
    for c, sl in enumerate(chunks):
        z_ref[:, :, sl] = out_z[c].reshape(n2, kb, LANES).astype(BF16)


def _dft_tables(s, n1, n2, gd):
    def cs(num, den):
        ang = (2.0 * math.pi / den) * (num % den).astype(F32)
        return jnp.cos(ang), jnp.sin(ang)

    ic = jnp.arange(gd, dtype=I32)
    cc, sc = cs(ic[:, None] * ic[None, :], gd)
    chan = jnp.concatenate([cc, -sc], axis=1).astype(BF16)
    i1 = jnp.arange(n1, dtype=I32)
    c1, s1 = cs(i1[:, None] * i1[None, :], n1)
    m1 = jnp.concatenate([jnp.concatenate([c1, s1], axis=1),
                          jnp.concatenate([-s1, c1], axis=1)], axis=0).astype(BF16)
    k1 = jnp.arange(n1, dtype=I32)[:, None, None]
    k2 = jnp.arange(n2, dtype=I32)[None, :, None]
    s2 = jnp.arange(n2, dtype=I32)[None, None, :]
    cg, sg = cs(((n1 * k2 + k1) % s) * s2, s)
    scale = 1.0 / math.sqrt(float(s) * float(gd))
    g = (jnp.concatenate([cg, sg], axis=2) * scale).astype(BF16)
    return chan, m1, g


def fourier_core(x, g_norm, w_in, batch, seq):
    n, d = x.shape
    gd = d // F_GROUPS
    n1 = 1 << ((seq.bit_length() - 1 + 1) // 2)
    n2 = seq // n1
    assert n1 * n2 == seq and n1 % 16 == 0 and n2 % 16 == 0
    chan, m1, gtab = _dft_tables(seq, n1, n2, gd)

    vr, vi = pl.pallas_call(
        _fourier_in_kernel,
        grid=(n // ROW_TILE,),
        in_specs=[pl.BlockSpec((ROW_TILE, d), lambda i: (i, 0)),
                  pl.BlockSpec((1, d), lambda i: (0, 0)),
                  pl.BlockSpec((d, d), lambda i: (0, 0)),
                  pl.BlockSpec((gd, 2 * gd), lambda i: (0, 0))],
        out_specs=[pl.BlockSpec((ROW_TILE, d), lambda i: (i, 0))] * 2,
        out_shape=[jax.ShapeDtypeStruct((n, d), BF16)] * 2,
        compiler_params=_cparams(("parallel",)),
    )(x, g_norm.reshape(1, d), w_in, chan)

    t2, dc = SUB, min(d, 512)
    a_spec = pl.BlockSpec((n1, t2, dc), lambda b, j, c: (b, j, c))
    yr, yi = pl.pallas_call(
        _dft_a_kernel,
        grid=(batch, n2 // t2, d // dc),
        in_specs=[a_spec, a_spec, pl.BlockSpec((2 * n1, 2 * n1), lambda b, j, c: (0, 0))],
        out_specs=[a_spec, a_spec],
        out_shape=[jax.ShapeDtypeStruct((batch * n1, n2, d), BF16)] * 2,
        scratch_shapes=[pltpu.VMEM((dc // LANES, n1 * t2, LANES), F32) for _ in range(4)],
        compiler_params=_cparams(("parallel", "parallel", "parallel")),
    )(vr.reshape(batch * n1, n2, d), vi.reshape(batch * n1, n2, d), m1)

    kb = SUB
    c_spec = pl.BlockSpec((kb, n2, d), lambda b, k: (b * (n1 // kb) + k, 0, 0))
    z = pl.pallas_call(
        _dft_c_kernel,
        grid=(batch, n1 // kb),
        in_specs=[c_spec, c_spec, pl.BlockSpec((kb, n2, 2 * n2), lambda b, k: (k, 0, 0))],
        out_specs=pl.BlockSpec((n2, kb, d), lambda b, k: (b, k, 0)),
        out_shape=jax.ShapeDtypeStruct((batch * n2, n1, d), BF16),
        scratch_shapes=[pltpu.VMEM((d // LANES, n2 * kb, LANES), F32)],
        compiler_params=_cparams(("parallel", "parallel")),
    )(yr, yi, gtab)
    return z.reshape(n, d)


def _attn_kernel(q_ref, kp_ref, kc_ref, kn_ref, vp_ref, vc_ref, vn_ref, bias_ref, o_ref):
    rq = q_ref.shape[0]
    q = q_ref[...] * jnp.asarray(HEAD_DIM ** -0.5, BF16)
    k = jnp.concatenate([kp_ref[...], kc_ref[...], kn_ref[...]], axis=0)
    v = jnp.concatenate([vp_ref[...], vc_ref[...], vn_ref[...]], axis=0)
    first = lax.broadcasted_iota(I32, (1, 2 * HEAD_DIM), 1) < HEAD_DIM
    for hp in range(N_HEADS // 2):
        sl = slice(hp * 2 * HEAD_DIM, (hp + 1) * 2 * HEAD_DIM)
        q2, k2, v2 = q[:, sl], k[:, sl], v[:, sl]
        zero = jnp.zeros_like(q2)
        qs = jnp.concatenate([jnp.where(first, q2, zero), jnp.where(first, zero, q2)], axis=0)
        sc = lax.dot_general(qs, k2, (((1,), (1,)), ((), ())), preferred_element_type=F32)
        sc = sc + bias_ref[0, hp]
        m = jnp.max(sc, axis=1, keepdims=True)
        p = jnp.exp(sc - m)
        l = jnp.sum(p, axis=1, keepdims=True)
        pv = jnp.dot(p.astype(BF16), v2, preferred_element_type=F32) / l
        o_ref[:, sl] = jnp.where(first, pv[:rq], pv[rq:]).astype(BF16)


def attention_tables(rpb):
    h = rpb.shape[0]
    wk = 3 * ATT_ROWS
    idx = np.clip(np.arange(2 * GRID_W - 1) - (GRID_W - NA_KW), 0, 2 * NA_KW - 2)
    v = jnp.concatenate([rpb[:, :, idx], jnp.zeros(rpb.shape[:2] + (1,), rpb.dtype)], axis=-1)
    m = jnp.tile(v, (1, 1, GRID_W))[:, :, :GRID_W * (2 * GRID_W - 1)]
    blk = m.reshape(h, 2 * NA_KH - 1, GRID_W, 2 * GRID_W - 1)[:, :, :, GRID_W - 1:]
    lo = NA_KH - 1 - ATT_ROWS
    per_row = jnp.stack([blk[:, lo - iq: lo - iq + wk] for iq in range(ATT_ROWS)], axis=1)
    bias = jnp.transpose(per_row, (0, 1, 3, 2, 4)).reshape(h, ATT_ROWS * GRID_W, wk * GRID_W).astype(F32)

    bias = bias.reshape(h // 2, 2 * ATT_ROWS * GRID_W, wk * GRID_W)

    iq = np.arange(ATT_ROWS)[:, None, None, None]
    c = np.arange(GRID_W)[None, :, None, None]
    kr = np.arange(wk)[None, None, :, None]
    kc = np.arange(GRID_W)[None, None, None, :]
    shape = (ATT_ROWS, GRID_W, wk, GRID_W)
    ws = np.clip(c - NA_KW // 2, 0, GRID_W - NA_KW)
    col_ok = (kc >= ws) & (kc < ws + NA_KW)
    row_ok = [
        (kr >= ATT_ROWS) & (kr < ATT_ROWS + NA_KH) & (iq >= 0),
        (kr >= iq) & (kr < iq + NA_KH),
        (kr < NA_KH) & (iq >= 0),
    ]
    mask = np.stack([np.where(np.broadcast_to(r & col_ok, shape), 0.0, NEG_INF).reshape(ATT_ROWS * GRID_W, -1)
                     for r in row_ok]).astype(np.float32)
    return bias[None] + jnp.asarray(np.tile(mask, (1, 2, 1)))[:, None]


def attention_core(qkv, tables, batch, seq):
    n = qkv.shape[0]
    d = N_HEADS * HEAD_DIM
    rows = seq // GRID_W
    groups = rows // ATT_ROWS
    assert rows % ATT_ROWS == 0 and rows >= 4 * ATT_ROWS and NA_KH == 2 * ATT_ROWS
    rq = ATT_ROWS * GRID_W
    bias = tables

    def blk(off, col):
        def index_map(b, g):
            return (b * groups + jnp.clip(g + off, 0, groups - 1), col)
        return pl.BlockSpec((rq, d), index_map)

    def case_map(b, g):
        return (jnp.where(g == 0, 0, jnp.where(g == groups - 1, 2, 1)), 0, 0, 0)

    return pl.pallas_call(
        _attn_kernel,
        grid=(batch, groups),
        in_specs=[blk(0, 0), blk(-1, 1), blk(0, 1), blk(1, 1), blk(-1, 2), blk(0, 2), blk(1, 2),
                  pl.BlockSpec((1, N_HEADS // 2, 2 * rq, 3 * rq), case_map)],
        out_specs=pl.BlockSpec((rq, d), lambda b, g: (b * groups + g, 0)),
        out_shape=jax.ShapeDtypeStruct((n, d), BF16),
        compiler_params=_cparams(("parallel", "parallel")),
    )(qkv, qkv, qkv, qkv, qkv, qkv, qkv, bias)


def _post_kernel(a_ref, x_ref, wo_ref, g_ref, wr_ref, xo_ref, h_ref, p_ref):
    xo_ref[...] = x_ref[...] + jnp.dot(a_ref[...], wo_ref[...], preferred_element_type=F32)
    for r in range(x_ref.shape[0] // LANES):
        rows = slice(r * LANES, (r + 1) * LANES)
        hf = _rms(xo_ref[rows, :], g_ref[...])
        h_hi = hf.astype(BF16)
        h_ref[rows, :] = h_hi
        lg = jnp.dot(h_hi, wr_ref[...], preferred_element_type=F32)
        logits = lg.T[:p_ref.shape[0]]
        m = jnp.max(logits, axis=0, keepdims=True)
        e = jnp.exp(logits - m)
        p_ref[:, rows] = e / jnp.sum(e, axis=0, keepdims=True)


def pad_router_weights(w_router):
    return jnp.pad(w_router.astype(BF16), ((0, 0), (0, LANES - w_router.shape[1])))


def post_mixer(a, x, w_o, g_ffn, w_router2, e):
    n, d = x.shape
    return pl.pallas_call(
        _post_kernel,
        grid=(n // ROW_TILE,),
        in_specs=[pl.BlockSpec((ROW_TILE, d), lambda i: (i, 0)),
                  pl.BlockSpec((ROW_TILE, d), lambda i: (i, 0)),
                  pl.BlockSpec((d, d), lambda i: (0, 0)),
                  pl.BlockSpec((1, d), lambda i: (0, 0)),
                  pl.BlockSpec((d, LANES), lambda i: (0, 0))],
        out_specs=[pl.BlockSpec((ROW_TILE, d), lambda i: (i, 0)),
                   pl.BlockSpec((ROW_TILE, d), lambda i: (i, 0)),
                   pl.BlockSpec((e, ROW_TILE), lambda i: (0, i))],
        out_shape=[jax.ShapeDtypeStruct((n, d), F32),
                   jax.ShapeDtypeStruct((n, d), BF16),
                   jax.ShapeDtypeStruct((e, n), F32)],
        compiler_params=_cparams(("parallel",)),
    )(a, x, w_o, g_ffn.reshape(1, d), w_router2)


def _threshold_kernel(p_ref, thr_ref, need_ref, *, cap):
    e = p_ref.shape[0]

    def count_ge(cand):
        bits = pltpu.bitcast(p_ref[...], I32)
        return jnp.sum((bits >= cand).astype(F32), axis=1, keepdims=True)

    def body(i, prefix):
        cand = prefix | jnp.left_shift(jnp.int32(1), 30 - i)
        return jnp.where(count_ge(cand) >= cap, cand, prefix)

    thr = lax.fori_loop(0, 31, body, jnp.zeros((e, 1), I32))
    n_gt = count_ge(thr + 1)
    thr_ref[...] = jnp.broadcast_to(thr, thr_ref.shape)
    need_ref[...] = jnp.broadcast_to(cap - n_gt, need_ref.shape)


def _select_kernel(p_ref, thr_ref, need_ref, ut_ref, gate_ref, pos_ref, cnt_ref, carry_ref):
    @pl.when(pl.program_id(0) == 0)
    def _():
        carry_ref[...] = jnp.zeros_like(carry_ref)

    thr = thr_ref[:, :1]
    ut = ut_ref[...]
    t = ut.shape[0]
    for i in range(cnt_ref.shape[0]):
        cols = slice(i * t, (i + 1) * t)
        p = p_ref[:, cols]
        bits = pltpu.bitcast(p, I32)
        gt = bits > thr
        eq = bits == thr
        eq_f = jnp.where(eq, 1.0, 0.0)
        eq_before = jnp.dot(eq_f.astype(BF16), ut, preferred_element_type=F32) + carry_ref[:, :1]
        sel = gt | (eq & (eq_before < need_ref[:, :1]))
        carry_ref[...] = carry_ref[...] + jnp.sum(eq_f, axis=1, keepdims=True)
        sel_f = jnp.where(sel, 1.0, 0.0)
        rank = jnp.dot(sel_f.astype(BF16), ut, preferred_element_type=F32)
        pos_ref[:, cols] = jnp.where(sel, rank.astype(I32), -1)
        gate_ref[:, cols] = jnp.where(sel, p, 0.0)
        cnt = jnp.sum(sel_f, axis=1, keepdims=True).astype(I32)
        cnt_ref[i] = jnp.broadcast_to(cnt, cnt_ref.shape[1:])


def expert_choice_select(probs_t):
    e, n = probs_t.shape
    cap = EC_CAPACITY * n // e
    j = n // MOE_T
    thr, need = pl.pallas_call(
        functools.partial(_threshold_kernel, cap=cap),
        out_shape=[jax.ShapeDtypeStruct((e, 128), I32), jax.ShapeDtypeStruct((e, 128), F32)],
        compiler_params=pltpu.CompilerParams(vmem_limit_bytes=VMEM_LIMIT),
    )(probs_t)
    it = np.arange(MOE_T)
    ut = jnp.asarray((it[:, None] < it[None, :]).astype(np.float32), BF16)
    tiles = _largest_tile(j, (8, 4, 2, 1))
    gates, pos, cnt = pl.pallas_call(
        _select_kernel,
        grid=(j // tiles,),
        in_specs=[pl.BlockSpec((e, tiles * MOE_T), lambda i: (0, i)),
                  pl.BlockSpec((e, 128), lambda i: (0, 0)),
                  pl.BlockSpec((e, 128), lambda i: (0, 0)),
                  pl.BlockSpec((MOE_T, MOE_T), lambda i: (0, 0))],
        out_specs=[pl.BlockSpec((e, tiles * MOE_T), lambda i: (0, i)),
                   pl.BlockSpec((e, tiles * MOE_T), lambda i: (0, i)),
                   pl.BlockSpec((tiles, e, 128), lambda i: (i, 0, 0))],
        out_shape=[jax.ShapeDtypeStruct((e, n), F32),
                   jax.ShapeDtypeStruct((e, n), I32),
                   jax.ShapeDtypeStruct((j, e, 128), I32)],
        scratch_shapes=[pltpu.VMEM((e, 128), F32)],
        compiler_params=_cparams(("arbitrary",)),
    )(probs_t, thr, need, ut)
    return gates, pos, cnt[:, :, 0].T


def _one_hot_slots(pos_ref, row_of_rank0, lo, hi):
    e, t = pos_ref.shape
    rows = lax.broadcasted_iota(I32, (MOE_W, t), 0)
    blocks = []
    for i in range(e):
        r = pos_ref[i:i + 1, :]
        tgt = jnp.where((r >= jnp.maximum(lo[i], 0)) & (r < hi[i]), r + row_of_rank0[i], -1)
        blocks.append(jnp.where(rows == tgt, 1.0, 0.0).astype(BF16))
    return jnp.concatenate(blocks, axis=0)


def _gather_kernel(s16_ref, off_ref, cpass_ref, coff_ref, npass_ref, h_ref, gate_ref, pos_ref, xs_ref,
                   ybuf, carry, zbuf, nbatch, sem, *, region, cap):
    j = pl.program_id(0)
    nj = pl.num_programs(0)
    e, t = pos_ref.shape
    d = h_ref.shape[1]

    @pl.when(j == 0)
    def _():
        nbatch[0] = 0
        carry[...] = jnp.zeros_like(carry)

    g = gate_ref[...]
    g_hi = g.astype(BF16)
    r1 = g - g_hi.astype(F32)
    g_mid = r1.astype(BF16)
    g_lo = (r1 - g_mid.astype(F32)).astype(BF16)
    g3 = jnp.concatenate([g_hi, g_mid, g_lo, jnp.zeros((GATE_COLS - 3 * e, t), BF16)], axis=0)

    def window_copy(slot, i, dst):
        return pltpu.make_async_copy(ybuf.at[slot, pl.ds(i * MOE_W, MOE_W)], xs_ref.at[pl.ds(dst, MOE_W)], sem.at[0])

    def wait_batch(slot):
        for i in range(e):
            window_copy(slot, i, 0).wait()

    def one_pass(p, carry_in):
        k = nbatch[0]
        slot = k % 2
        off = [off_ref[i * nj + j] for i in range(e)]
        shift = [off[i] - p * MOE_W for i in range(e)]
        onehot = _one_hot_slots(pos_ref, shift, [-shift[i] for i in range(e)], [MOE_W - shift[i] for i in range(e)])
        ybuf[slot, :, :d] = jnp.dot(onehot, h_ref[...], preferred_element_type=F32).astype(BF16)
        ybuf[slot, :, d:] = lax.dot_general(onehot, g3, (((1,), (1,)), ((), ())),
                                            preferred_element_type=F32).astype(BF16)
        for i in range(e):
            head = pl.ds(i * MOE_W, SUB)
            kept = pl.ds(i * SUB, SUB)
            add = jnp.where((p == 0) & (off[i] > 0), carry[kept, :].astype(F32), 0.0)
            ybuf[slot, head, :] = (ybuf[slot, head, :].astype(F32) + add).astype(BF16)
            src = pl.multiple_of(i * MOE_W + coff_ref[i * nj + j], SUB)
            carry[kept, :] = jnp.where(p == cpass_ref[i * nj + j], ybuf[slot, pl.ds(src, SUB), :], carry[kept, :])

        @pl.when(k > 0)
        def _():
            wait_batch(1 - slot)

        for i in range(e):
            dst = pl.multiple_of(i * region + s16_ref[i * nj + j] + p * MOE_W, SUB)
            window_copy(slot, i, dst).start()
        nbatch[0] = k + 1
        return carry_in

    lax.fori_loop(0, npass_ref[j], one_pass, 0)

    @pl.when(j == nj - 1)
    def _():
        wait_batch((nbatch[0] - 1) % 2)
        zbuf[...] = jnp.zeros_like(zbuf)
        zr = zbuf.shape[0]
        cps = [pltpu.make_async_copy(zbuf, xs_ref.at[pl.ds(i * region + cap + c * zr, zr)], sem.at[1])
               for i in range(e) for c in range((region - cap) // zr)]
        for cp in cps:
            cp.start()
        for cp in cps:
            cp.wait()


def _ffn_kernel(x_ref, wg_ref, wu_ref, wd_ref, o_ref, acc_ref):
    ei, f = pl.program_id(0), pl.program_id(2)
    d = o_ref.shape[1]

    @pl.when(f == 0)
    def _():
        acc_ref[...] = jnp.zeros_like(acc_ref)

    x = x_ref[:, :d]
    gate = jnp.dot(x, wg_ref[0], preferred_element_type=F32)
    up = jnp.dot(x, wu_ref[0], preferred_element_type=F32)
    hid = (gate * jax.nn.sigmoid(gate) * up).astype(BF16)
    acc_ref[...] += jnp.dot(hid, wd_ref[0], preferred_element_type=F32)

    @pl.when(f == pl.num_programs(2) - 1)
    def _():
        n_e = pl.num_programs(0)
        lane = lax.broadcasted_iota(I32, (1, GATE_COLS), 1)
        own = (lane < 3 * n_e) & (lane % n_e == ei)
        g = jnp.sum(jnp.where(own, x_ref[:, d:].astype(F32), 0.0), axis=1, keepdims=True)
        o_ref[...] = (acc_ref[...] * g).astype(BF16)


def _combine_kernel(a_ref, s16_ref, npass_ref, x_ref, pos_ref, gfin_ref, o_hbm, out_ref, obuf, sem,
                    *, region, cap, final_norm):
    j = pl.program_id(0)
    nj = pl.num_programs(0)
    e = pos_ref.shape[0]
    slot = j % 2

    def window_start(i, jj, p):
        return jnp.minimum(s16_ref[i * nj + jj] + p * MOE_W, cap - MOE_W)

    def window_copy(i, src, buf):
        return pltpu.make_async_copy(o_hbm.at[pl.ds(src, MOE_W)], obuf.at[buf, pl.ds(i * MOE_W, MOE_W)], sem.at[buf])

    def fetch(jj, p, buf):
        for i in range(e):
            window_copy(i, pl.multiple_of(i * region + window_start(i, jj, p), SUB), buf).start()

    def wait(buf):
        for i in range(e):
            window_copy(i, 0, buf).wait()

    def one_hot(p):
        lo, hi, row0 = [], [], []
        for i in range(e):
            a = a_ref[i * nj + j]
            first = s16_ref[i * nj + j] + p * MOE_W - a
            lo.append(first)
            hi.append(first + MOE_W)
            row0.append(a - window_start(i, j, p))
        return _one_hot_slots(pos_ref, row0, lo, hi)

    @pl.when(j == 0)
    def _():
        fetch(0, 0, 0)

    @pl.when(j + 1 < nj)
    def _():
        fetch(j + 1, 0, 1 - slot)

    onehot = one_hot(0)
    wait(slot)
    y = x_ref[...] + lax.dot_general(onehot, obuf[slot], (((0,), (0,)), ((), ())), preferred_element_type=F32)

    def extra_pass(p, acc):
        fetch(j, p, 2)
        oh = one_hot(p)
        wait(2)
        return acc + lax.dot_general(oh, obuf[2], (((0,), (0,)), ((), ())), preferred_element_type=F32)

    y = lax.fori_loop(1, npass_ref[j], extra_pass, y)
    out_ref[...] = _rms(y, gfin_ref[...]) if final_norm else y


def _largest_tile(n, options):
    for t in options:
        if n % t == 0:
            return t
    raise ValueError(f"no tile in {options} divides {n}")


def expert_choice_moe(x, h, probs_t, w_gate, w_up, w_down, layer, g_final=None):
    n, d = x.shape
    e = probs_t.shape[0]
    f = w_gate.shape[3]
    cap = EC_CAPACITY * n // e
    nj = n // MOE_T
    da = d + GATE_COLS
    assert 3 * e <= GATE_COLS and cap % SUB == 0 and cap >= MOE_W
    gates, pos, cnt = expert_choice_select(probs_t)

    ends = jnp.cumsum(cnt, axis=1)
    a = ends - cnt
    off = a % SUB
    s16 = a - off
    tot = off + cnt
    group = tot // SUB * SUB
    cpass = jnp.where(tot % SUB != 0, group // MOE_W, -1)
    coff = group % MOE_W
    npass = jnp.maximum(1, jnp.max((tot + MOE_W - 1) // MOE_W, axis=0))
    flat = lambda v: v.astype(I32).reshape(-1)

    tm = _largest_tile(cap, (1024, 512, 256, 128))
    zr = 256
    region = cap + -(-(MOE_T + 2 * MOE_W) // max(tm, zr)) * max(tm, zr)
    mt = cap // tm

    xs = pl.pallas_call(
        functools.partial(_gather_kernel, region=region, cap=cap),
        grid_spec=pltpu.PrefetchScalarGridSpec(
            num_scalar_prefetch=5,
            grid=(nj,),
            in_specs=[pl.BlockSpec((MOE_T, d), lambda i, *_: (i, 0)),
                      pl.BlockSpec((e, MOE_T), lambda i, *_: (0, i)),
                      pl.BlockSpec((e, MOE_T), lambda i, *_: (0, i))],
            out_specs=pl.BlockSpec(memory_space=pl.ANY),
            scratch_shapes=[pltpu.VMEM((2, e * MOE_W, da), BF16),
                            pltpu.VMEM((e * SUB, da), BF16),
                            pltpu.VMEM((zr, da), BF16),
                            pltpu.SMEM((1,), I32),
                            pltpu.SemaphoreType.DMA((2,))]),
        out_shape=jax.ShapeDtypeStruct((e * region, da), BF16),
        compiler_params=_cparams(("arbitrary",)),
    )(flat(s16), flat(off), flat(cpass), flat(coff), npass.astype(I32), h, gates, pos)

    fc = f // 2 if (f > 1024 and (f // 2) % 128 == 0) else f
    ft = f // fc
    rt = region // tm
    out_slots = pl.pallas_call(
        _ffn_kernel,
        grid=(e, mt, ft),
        in_specs=[pl.BlockSpec((tm, da), lambda ei, m, fi: (ei * rt + m, 0)),
                  pl.BlockSpec((None, 1, d, fc), lambda ei, m, fi: (layer, ei, 0, fi)),
                  pl.BlockSpec((None, 1, d, fc), lambda ei, m, fi: (layer, ei, 0, fi)),
                  pl.BlockSpec((None, 1, fc, d), lambda ei, m, fi: (layer, ei, fi, 0))],
        out_specs=pl.BlockSpec((tm, d), lambda ei, m, fi: (ei * mt + m, 0)),
        out_shape=jax.ShapeDtypeStruct((e * cap, d), BF16),
        scratch_shapes=[pltpu.VMEM((tm, d), F32)],
        compiler_params=_cparams(("parallel", "parallel", "arbitrary")),
    )(xs, w_gate, w_up, w_down)

    final_norm = g_final is not None
    gfin = (g_final if final_norm else jnp.ones((d,), F32)).reshape(1, d)
    return pl.pallas_call(
        functools.partial(_combine_kernel, region=cap, cap=cap, final_norm=final_norm),
        grid_spec=pltpu.PrefetchScalarGridSpec(
            num_scalar_prefetch=3,
            grid=(nj,),
            in_specs=[pl.BlockSpec((MOE_T, d), lambda i, *_: (i, 0)),
                      pl.BlockSpec((e, MOE_T), lambda i, *_: (0, i)),
                      pl.BlockSpec((1, d), lambda i, *_: (0, 0)),
                      pl.BlockSpec(memory_space=pl.ANY)],
            out_specs=pl.BlockSpec((MOE_T, d), lambda i, *_: (i, 0)),
            scratch_shapes=[pltpu.VMEM((3, e * MOE_W, d), BF16),
                            pltpu.SemaphoreType.DMA((3,))]),
        out_shape=jax.ShapeDtypeStruct((n, d), F32),
        compiler_params=_cparams(("arbitrary",)),
    )(flat(a), flat(s16), npass.astype(I32), x, pos, gfin, out_slots)


def _trunk(x3, params):
    batch, seq, d = x3.shape
    x = x3.reshape(batch * seq, d)
    depth = params["norm_mix"].shape[0]
    for i in range(depth):
        jm = i // 2
        if i % 2 == 0:
            a = fourier_core(x, params["norm_mix"][i], params["w_fourier_in"][jm], batch, seq)
            w_o = params["w_fourier_out"][jm]
        else:
            qkv = norm_matmul(x, params["norm_mix"][i], params["w_qkv"][jm])
            a = attention_core(qkv, params["attn_tables"][jm], batch, seq)
            w_o = params["w_attn_out"][jm]
        x, h, probs_t = post_mixer(a, x, w_o, params["norm_ffn"][i], params["w_router2"][i], params["n_experts"])
        g_final = params["norm_final"] if i == depth - 1 else None
        x = expert_choice_moe(x, h, probs_t, params["w_gate"], params["w_up"], params["w_down"], i, g_final)
    return x.reshape(batch, seq, d)


def kernel(x_prompt, x_sample, norm_mix, norm_ffn, norm_final, w_fourier_in, w_fourier_out, w_qkv, w_attn_out,
           rel_pos_bias, w_router, w_gate, w_up, w_down):
    params = dict(
        norm_mix=norm_mix, norm_ffn=norm_ffn, norm_final=norm_final,
        attn_tables=[attention_tables(rel_pos_bias[i]) for i in range(rel_pos_bias.shape[0])],
        w_fourier_in=w_fourier_in.astype(BF16), w_fourier_out=w_fourier_out.astype(BF16),
        w_qkv=w_qkv.astype(BF16), w_attn_out=w_attn_out.astype(BF16),
        w_router2=[pad_router_weights(w_router[i]) for i in range(w_router.shape[0])],
        n_experts=w_router.shape[2],
        w_gate=w_gate.astype(BF16), w_up=w_up.astype(BF16), w_down=w_down.astype(BF16))
    return (_trunk(x_prompt, params), _trunk(x_sample, params))
```

```python
import functools
import math

import numpy as np
import jax
import jax.numpy as jnp
from jax import lax
from jax.experimental import pallas as pl
from jax.experimental.pallas import tpu as pltpu

EPS = 1e-6
GRID_W = 64
N_HEADS = 16
HEAD_DIM = 64
NA_KH = 8
NA_KW = 16
F_GROUPS = 4
EC_CAPACITY = 2
NEG_INF = -1e30

ROW_TILE = 1024
ATT_ROWS = 4
MOE_T = 256
MOE_W = 64
SUB = 16
LANES = 128
GATE_COLS = 128
VMEM_LIMIT = 56 * 1024 * 1024

F32 = jnp.float32
BF16 = jnp.bfloat16
I32 = jnp.int32


def _cparams(sem):
    return pltpu.CompilerParams(dimension_semantics=sem, vmem_limit_bytes=VMEM_LIMIT)


def _rms(x, g):
    r = lax.rsqrt(jnp.mean(x * x, axis=-1, keepdims=True) + EPS)
    return (x * r) * g


def _norm_matmul_kernel(x_ref, g_ref, w_ref, o_ref, *, col_chunk):
    h = _rms(x_ref[...], g_ref[...]).astype(BF16)
    for c in range(o_ref.shape[1] // col_chunk):
        sl = slice(c * col_chunk, (c + 1) * col_chunk)
        o_ref[:, sl] = jnp.dot(h, w_ref[:, sl], preferred_element_type=F32).astype(o_ref.dtype)


def norm_matmul(x, g, w):
    n, d = x.shape
    m = w.shape[1]
    return pl.pallas_call(
        functools.partial(_norm_matmul_kernel, col_chunk=min(m, 1024)),
        grid=(n // ROW_TILE,),
        in_specs=[pl.BlockSpec((ROW_TILE, d), lambda i: (i, 0)),
                  pl.BlockSpec((1, d), lambda i: (0, 0)),
                  pl.BlockSpec((d, m), lambda i: (0, 0))],
        out_specs=pl.BlockSpec((ROW_TILE, m), lambda i: (i, 0)),
        out_shape=jax.ShapeDtypeStruct((n, m), BF16),
        compiler_params=_cparams(("parallel",)),
    )(x, g.reshape(1, d), w)


def _fourier_in_kernel(x_ref, g_ref, w_ref, cs_ref, vr_ref, vi_ref):
    h = _rms(x_ref[...], g_ref[...]).astype(BF16)
    u = jnp.dot(h, w_ref[...], preferred_element_type=F32).astype(BF16)
    gd = cs_ref.shape[0]
    for gi in range(u.shape[1] // gd):
        sl = slice(gi * gd, (gi + 1) * gd)
        ab = jnp.dot(u[:, sl], cs_ref[...], preferred_element_type=F32)
        vr_ref[:, sl] = ab[:, :gd].astype(BF16)
        vi_ref[:, sl] = ab[:, gd:].astype(BF16)


def _dft_a_kernel(vr_ref, vi_ref, m_ref, yr_ref, yi_ref, in_r, in_i, out_r, out_i):
    n1, t2, dc = vr_ref.shape
    chunks = [slice(c * LANES, (c + 1) * LANES) for c in range(dc // LANES)]
    for c, sl in enumerate(chunks):
        in_r[c] = vr_ref[:, :, sl].astype(F32).reshape(n1 * t2, LANES)
        in_i[c] = vi_ref[:, :, sl].astype(F32).reshape(n1 * t2, LANES)
    for s in range(t2):
        rows = pl.ds(s, n1, stride=t2)
        v = jnp.concatenate([jnp.concatenate([ref[c, rows, :] for c in range(len(chunks))], axis=1)
                             for ref in (in_r, in_i)], axis=0).astype(BF16)
        y = jnp.dot(m_ref[...], v, preferred_element_type=F32)
        for c, sl in enumerate(chunks):
            out_r[c, rows, :] = y[:n1, sl]
            out_i[c, rows, :] = y[n1:, sl]
    for c, sl in enumerate(chunks):
        yr_ref[:, :, sl] = out_r[c].reshape(n1, t2, LANES).astype(BF16)
        yi_ref[:, :, sl] = out_i[c].reshape(n1, t2, LANES).astype(BF16)


def _dft_c_kernel(yr_ref, yi_ref, g_ref, z_ref, out_z):
    kb, n2, d = yr_ref.shape
    chunks = [slice(c * LANES, (c + 1) * LANES) for c in range(d // LANES)]
    for i in range(kb):
        ycat = jnp.concatenate([yr_ref[i], yi_ref[i]], axis=0)
        z = jnp.dot(g_ref[i], ycat, preferred_element_type=F32)
        for c, sl in enumerate(chunks):
            out_z[c, pl.ds(i, n2, stride=kb), :] = z[:, sl]
    for c, sl in enumerate(chunks):
        z_ref[:, :, sl] = out_z[c].reshape(n2, kb, LANES).astype(BF16)


def _dft_tables(s, n1, n2, gd):
    def cs(num, den):
        ang = (2.0 * math.pi / den) * (num % den).astype(F32)
        return jnp.cos(ang), jnp.sin(ang)

    ic = jnp.arange(gd, dtype=I32)
    cc, sc = cs(ic[:, None] * ic[None, :], gd)
    chan = jnp.concatenate([cc, -sc], axis=1).astype(BF16)
    i1 = jnp.arange(n1, dtype=I32)
    c1, s1 = cs(i1[:, None] * i1[None, :], n1)
    m1 = jnp.concatenate([jnp.concatenate([c1, s1], axis=1),
                          jnp.concatenate([-s1, c1], axis=1)], axis=0).astype(BF16)
    k1 = jnp.arange(n1, dtype=I32)[:, None, None]
    k2 = jnp.arange(n2, dtype=I32)[None, :, None]
    s2 = jnp.arange(n2, dtype=I32)[None, None, :]
    cg, sg = cs(((n1 * k2 + k1) % s) * s2, s)
    scale = 1.0 / math.sqrt(float(s) * float(gd))
    g = (jnp.concatenate([cg, sg], axis=2) * scale).astype(BF16)
    return chan, m1, g


def fourier_core(x, g_norm, w_in, batch, seq):
    n, d = x.shape
    gd = d // F_GROUPS
    n1 = 1 << ((seq.bit_length() - 1 + 1) // 2)
    n2 = seq // n1
    assert n1 * n2 == seq and n1 % 16 == 0 and n2 % 16 == 0
    chan, m1, gtab = _dft_tables(seq, n1, n2, gd)

    vr, vi = pl.pallas_call(
        _fourier_in_kernel,
        grid=(n // ROW_TILE,),
        in_specs=[pl.BlockSpec((ROW_TILE, d), lambda i: (i, 0)),
                  pl.BlockSpec((1, d), lambda i: (0, 0)),
                  pl.BlockSpec((d, d), lambda i: (0, 0)),
                  pl.BlockSpec((gd, 2 * gd), lambda i: (0, 0))],
        out_specs=[pl.BlockSpec((ROW_TILE, d), lambda i: (i, 0))] * 2,
        out_shape=[jax.ShapeDtypeStruct((n, d), BF16)] * 2,
        compiler_params=_cparams(("parallel",)),
    )(x, g_norm.reshape(1, d), w_in, chan)

    t2, dc = SUB, min(d, 512)
    a_spec = pl.BlockSpec((n1, t2, dc), lambda b, j, c: (b, j, c))
    yr, yi = pl.pallas_call(
        _dft_a_kernel,
        grid=(batch, n2 // t2, d // dc),
        in_specs=[a_spec, a_spec, pl.BlockSpec((2 * n1, 2 * n1), lambda b, j, c: (0, 0))],
        out_specs=[a_spec, a_spec],
        out_shape=[jax.ShapeDtypeStruct((batch * n1, n2, d), BF16)] * 2,
        scratch_shapes=[pltpu.VMEM((dc // LANES, n1 * t2, LANES), F32) for _ in range(4)],
        compiler_params=_cparams(("parallel", "parallel", "parallel")),
    )(vr.reshape(batch * n1, n2, d), vi.reshape(batch * n1, n2, d), m1)

    kb = SUB
    c_spec = pl.BlockSpec((kb, n2, d), lambda b, k: (b * (n1 // kb) + k, 0, 0))
    z = pl.pallas_call(
        _dft_c_kernel,
        grid=(batch, n1 // kb),
        in_specs=[c_spec, c_spec, pl.BlockSpec((kb, n2, 2 * n2), lambda b, k: (k, 0, 0))],
        out_specs=pl.BlockSpec((n2, kb, d), lambda b, k: (b, k, 0)),
        out_shape=jax.ShapeDtypeStruct((batch * n2, n1, d), BF16),
        scratch_shapes=[pltpu.VMEM((d // LANES, n2 * kb, LANES), F32)],
        compiler_params=_cparams(("parallel", "parallel")),
    )(yr, yi, gtab)
    return z.reshape(n, d)


def _attn_kernel(q_ref, kp_ref, kc_ref, kn_ref, vp_ref, vc_ref, vn_ref, bias_ref, o_ref):
    rq = q_ref.shape[0]
    q = q_ref[...] * jnp.asarray(HEAD_DIM ** -0.5, BF16)
    k = jnp.concatenate([kp_ref[...], kc_ref[...], kn_ref[...]], axis=0)
    v = jnp.concatenate([vp_ref[...], vc_ref[...], vn_ref[...]], axis=0)
    first = lax.broadcasted_iota(I32, (1, 2 * HEAD_DIM), 1) < HEAD_DIM
    for hp in range(N_HEADS // 2):
        sl = slice(hp * 2 * HEAD_DIM, (hp + 1) * 2 * HEAD_DIM)
        q2, k2, v2 = q[:, sl], k[:, sl], v[:, sl]
        zero = jnp.zeros_like(q2)
        qs = jnp.concatenate([jnp.where(first, q2, zero), jnp.where(first, zero, q2)], axis=0)
        sc = lax.dot_general(qs, k2, (((1,), (1,)), ((), ())), preferred_element_type=F32)
        sc = sc + bias_ref[0, hp]
        m = jnp.max(sc, axis=1, keepdims=True)
        p = jnp.exp(sc - m)
        l = jnp.sum(p, axis=1, keepdims=True)
        pv = jnp.dot(p.astype(BF16), v2, preferred_element_type=F32) / l
        o_ref[:, sl] = jnp.where(first, pv[:rq], pv[rq:]).astype(BF16)


def attention_tables(rpb):
    h = rpb.shape[0]
    wk = 3 * ATT_ROWS
    idx = np.clip(np.arange(2 * GRID_W - 1) - (GRID_W - NA_KW), 0, 2 * NA_KW - 2)
    v = jnp.concatenate([rpb[:, :, idx], jnp.zeros(rpb.shape[:2] + (1,), rpb.dtype)], axis=-1)
    m = jnp.tile(v, (1, 1, GRID_W))[:, :, :GRID_W * (2 * GRID_W - 1)]
    blk = m.reshape(h, 2 * NA_KH - 1, GRID_W, 2 * GRID_W - 1)[:, :, :, GRID_W - 1:]
    lo = NA_KH - 1 - ATT_ROWS
    per_row = jnp.stack([blk[:, lo - iq: lo - iq + wk] for iq in range(ATT_ROWS)], axis=1)
    bias = jnp.transpose(per_row, (0, 1, 3, 2, 4)).reshape(h, ATT_ROWS * GRID_W, wk * GRID_W).astype(F32)

    bias = bias.reshape(h // 2, 2 * ATT_ROWS * GRID_W, wk * GRID_W)

    iq = np.arange(ATT_ROWS)[:, None, None, None]
    c = np.arange(GRID_W)[None, :, None, None]
    kr = np.arange(wk)[None, None, :, None]
    kc = np.arange(GRID_W)[None, None, None, :]
    shape = (ATT_ROWS, GRID_W, wk, GRID_W)
    ws = np.clip(c - NA_KW // 2, 0, GRID_W - NA_KW)
    col_ok = (kc >= ws) & (kc < ws + NA_KW)
    row_ok = [
        (kr >= ATT_ROWS) & (kr < ATT_ROWS + NA_KH) & (iq >= 0),
        (kr >= iq) & (kr < iq + NA_KH),
        (kr < NA_KH) & (iq >= 0),
    ]
    mask = np.stack([np.where(np.broadcast_to(r & col_ok, shape), 0.0, NEG_INF).reshape(ATT_ROWS * GRID_W, -1)
                     for r in row_ok]).astype(np.float32)
    return bias[None] + jnp.asarray(np.tile(mask, (1, 2, 1)))[:, None]


def attention_core(qkv, tables, batch, seq):
    n = qkv.shape[0]
    d = N_HEADS * HEAD_DIM
    rows = seq // GRID_W
    groups = rows // ATT_ROWS
    assert rows % ATT_ROWS == 0 and rows >= 4 * ATT_ROWS and NA_KH == 2 * ATT_ROWS
    rq = ATT_ROWS * GRID_W
    bias = tables

    def blk(off, col):
        def index_map(b, g):
            return (b * groups + jnp.clip(g + off, 0, groups - 1), col)
        return pl.BlockSpec((rq, d), index_map)

    def case_map(b, g):
        return (jnp.where(g == 0, 0, jnp.where(g == groups - 1, 2, 1)), 0, 0, 0)

    return pl.pallas_call(
        _attn_kernel,
        grid=(batch, groups),
        in_specs=[blk(0, 0), blk(-1, 1), blk(0, 1), blk(1, 1), blk(-1, 2), blk(0, 2), blk(1, 2),
                  pl.BlockSpec((1, N_HEADS // 2, 2 * rq, 3 * rq), case_map)],
        out_specs=pl.BlockSpec((rq, d), lambda b, g: (b * groups + g, 0)),
        out_shape=jax.ShapeDtypeStruct((n, d), BF16),
        compiler_params=_cparams(("parallel", "parallel")),
    )(qkv, qkv, qkv, qkv, qkv, qkv, qkv, bias)


def _post_kernel(a_ref, x_ref, wo_ref, g_ref, wr_ref, xo_ref, h_ref, p_ref):
    xo_ref[...] = x_ref[...] + jnp.dot(a_ref[...], wo_ref[...], preferred_element_type=F32)
    for r in range(x_ref.shape[0] // LANES):
        rows = slice(r * LANES, (r + 1) * LANES)
        hf = _rms(xo_ref[rows, :], g_ref[...])
        h_hi = hf.astype(BF16)
        h_ref[rows, :] = h_hi
        lg = jnp.dot(h_hi, wr_ref[...], preferred_element_type=F32)
        logits = lg.T[:p_ref.shape[0]]
        m = jnp.max(logits, axis=0, keepdims=True)
        e = jnp.exp(logits - m)
        p_ref[:, rows] = e / jnp.sum(e, axis=0, keepdims=True)


def pad_router_weights(w_router):
    return jnp.pad(w_router.astype(BF16), ((0, 0), (0, LANES - w_router.shape[1])))


def post_mixer(a, x, w_o, g_ffn, w_router2, e):
    n, d = x.shape
    return pl.pallas_call(
        _post_kernel,
        grid=(n // ROW_TILE,),
        in_specs=[pl.BlockSpec((ROW_TILE, d), lambda i: (i, 0)),
                  pl.BlockSpec((ROW_TILE, d), lambda i: (i, 0)),
                  pl.BlockSpec((d, d), lambda i: (0, 0)),
                  pl.BlockSpec((1, d), lambda i: (0, 0)),
                  pl.BlockSpec((d, LANES), lambda i: (0, 0))],
        out_specs=[pl.BlockSpec((ROW_TILE, d), lambda i: (i, 0)),
                   pl.BlockSpec((ROW_TILE, d), lambda i: (i, 0)),
                   pl.BlockSpec((e, ROW_TILE), lambda i: (0, i))],
        out_shape=[jax.ShapeDtypeStruct((n, d), F32),
                   jax.ShapeDtypeStruct((n, d), BF16),
                   jax.ShapeDtypeStruct((e, n), F32)],
        compiler_params=_cparams(("parallel",)),
    )(a, x, w_o, g_ffn.reshape(1, d), w_router2)


def _threshold_kernel(p_ref, thr_ref, need_ref, *, cap):
    e = p_ref.shape[0]

    def count_ge(cand):
        bits = pltpu.bitcast(p_ref[...], I32)
        return jnp.sum((bits >= cand).astype(F32), axis=1, keepdims=True)

    def body(i, prefix):
        cand = prefix | jnp.left_shift(jnp.int32(1), 30 - i)
        return jnp.where(count_ge(cand) >= cap, cand, prefix)

    thr = lax.fori_loop(0, 31, body, jnp.zeros((e, 1), I32))
    n_gt = count_ge(thr + 1)
    thr_ref[...] = jnp.broadcast_to(thr, thr_ref.shape)
    need_ref[...] = jnp.broadcast_to(cap - n_gt, need_ref.shape)


def _select_kernel(p_ref, thr_ref, need_ref, ut_ref, gate_ref, pos_ref, cnt_ref, carry_ref):
    @pl.when(pl.program_id(0) == 0)
    def _():
        carry_ref[...] = jnp.zeros_like(carry_ref)

    thr = thr_ref[:, :1]
    ut = ut_ref[...]
    t = ut.shape[0]
    for i in range(cnt_ref.shape[0]):
        cols = slice(i * t, (i + 1) * t)
        p = p_ref[:, cols]
        bits = pltpu.bitcast(p, I32)
        gt = bits > thr
        eq = bits == thr
        eq_f = jnp.where(eq, 1.0, 0.0)
        eq_before = jnp.dot(eq_f.astype(BF16), ut, preferred_element_type=F32) + carry_ref[:, :1]
        sel = gt | (eq & (eq_before < need_ref[:, :1]))
        carry_ref[...] = carry_ref[...] + jnp.sum(eq_f, axis=1, keepdims=True)
        sel_f = jnp.where(sel, 1.0, 0.0)
        rank = jnp.dot(sel_f.astype(BF16), ut, preferred_element_type=F32)
        pos_ref[:, cols] = jnp.where(sel, rank.astype(I32), -1)
        gate_ref[:, cols] = jnp.where(sel, p, 0.0)
        cnt = jnp.sum(sel_f, axis=1, keepdims=True).astype(I32)
        cnt_ref[i] = jnp.broadcast_to(cnt, cnt_ref.shape[1:])


def expert_choice_select(probs_t):
    e, n = probs_t.shape
    cap = EC_CAPACITY * n // e
    j = n // MOE_T
    thr, need = pl.pallas_call(
        functools.partial(_threshold_kernel, cap=cap),
        out_shape=[jax.ShapeDtypeStruct((e, 128), I32), jax.ShapeDtypeStruct((e, 128), F32)],
        compiler_params=pltpu.CompilerParams(vmem_limit_bytes=VMEM_LIMIT),
    )(probs_t)
    it = np.arange(MOE_T)
    ut = jnp.asarray((it[:, None] < it[None, :]).astype(np.float32), BF16)
    tiles = _largest_tile(j, (8, 4, 2, 1))
    gates, pos, cnt = pl.pallas_call(
        _select_kernel,
        grid=(j // tiles,),
        in_specs=[pl.BlockSpec((e, tiles * MOE_T), lambda i: (0, i)),
                  pl.BlockSpec((e, 128), lambda i: (0, 0)),
                  pl.BlockSpec((e, 128), lambda i: (0, 0)),
                  pl.BlockSpec((MOE_T, MOE_T), lambda i: (0, 0))],
        out_specs=[pl.BlockSpec((e, tiles * MOE_T), lambda i: (0, i)),
                   pl.BlockSpec((e, tiles * MOE_T), lambda i: (0, i)),
                   pl.BlockSpec((tiles, e, 128), lambda i: (i, 0, 0))],
        out_shape=[jax.ShapeDtypeStruct((e, n), F32),
                   jax.ShapeDtypeStruct((e, n), I32),
                   jax.ShapeDtypeStruct((j, e, 128), I32)],
        scratch_shapes=[pltpu.VMEM((e, 128), F32)],
        compiler_params=_cparams(("arbitrary",)),
    )(probs_t, thr, need, ut)
    return gates, pos, cnt[:, :, 0].T


def _one_hot_slots(pos_ref, row_of_rank0, lo, hi):
    e, t = pos_ref.shape
    rows = lax.broadcasted_iota(I32, (MOE_W, t), 0)
    blocks = []
    for i in range(e):
        r = pos_ref[i:i + 1, :]
        tgt = jnp.where((r >= jnp.maximum(lo[i], 0)) & (r < hi[i]), r + row_of_rank0[i], -1)
        blocks.append(jnp.where(rows == tgt, 1.0, 0.0).astype(BF16))
    return jnp.concatenate(blocks, axis=0)


def _gather_kernel(s16_ref, off_ref, cpass_ref, coff_ref, npass_ref, h_ref, gate_ref, pos_ref, xs_ref,
                   ybuf, carry, zbuf, nbatch, sem, *, region, cap, nj):
    e = pos_ref.shape[0]
    t = MOE_T
    d = h_ref.shape[1]
    tiles = pos_ref.shape[1] // t

    @pl.when(pl.program_id(0) == 0)
    def _():
        nbatch[0] = 0
        carry[...] = jnp.zeros_like(carry)

    def window_copy(slot, i, dst):
        return pltpu.make_async_copy(ybuf.at[slot, pl.ds(i * MOE_W, MOE_W)], xs_ref.at[pl.ds(dst, MOE_W)], sem.at[0])

    def wait_batch(slot):
        for i in range(e):
            window_copy(slot, i, 0).wait()

    for u in range(tiles):
        _gather_tile(pl.program_id(0) * tiles + u, nj, region,
                     s16_ref, off_ref, cpass_ref, coff_ref, npass_ref,
                     h_ref.at[pl.ds(u * t, t), :], gate_ref[:, u * t:(u + 1) * t], pos_ref.at[:, pl.ds(u * t, t)],
                     ybuf, carry, nbatch, window_copy, wait_batch)

    @pl.when(pl.program_id(0) == pl.num_programs(0) - 1)
    def _():
        wait_batch((nbatch[0] - 1) % 2)
        zbuf[...] = jnp.zeros_like(zbuf)
        zr = zbuf.shape[0]
        cps = [pltpu.make_async_copy(zbuf, xs_ref.at[pl.ds(i * region + cap + c * zr, zr)], sem.at[1])
               for i in range(e) for c in range((region - cap) // zr)]
        for cp in cps:
            cp.start()
        for cp in cps:
            cp.wait()


def _gather_tile(j, nj, region, s16_ref, off_ref, cpass_ref, coff_ref, npass_ref, h_ref, g, pos_ref,
                 ybuf, carry, nbatch, window_copy, wait_batch):
    e, t = pos_ref.shape
    d = h_ref.shape[1]
    g_hi = g.astype(BF16)
    r1 = g - g_hi.astype(F32)
    g_mid = r1.astype(BF16)
    g_lo = (r1 - g_mid.astype(F32)).astype(BF16)
    g3 = jnp.concatenate([g_hi, g_mid, g_lo, jnp.zeros((GATE_COLS - 3 * e, t), BF16)], axis=0)

    def one_pass(p, carry_in):
        k = nbatch[0]
        slot = k % 2
        off = [off_ref[i * nj + j] for i in range(e)]
        shift = [off[i] - p * MOE_W for i in range(e)]
        onehot = _one_hot_slots(pos_ref, shift, [-shift[i] for i in range(e)], [MOE_W - shift[i] for i in range(e)])
        ybuf[slot, :, :d] = jnp.dot(onehot, h_ref[...], preferred_element_type=F32).astype(BF16)
        ybuf[slot, :, d:] = lax.dot_general(onehot, g3, (((1,), (1,)), ((), ())),
                                            preferred_element_type=F32).astype(BF16)
        for i in range(e):
            head = pl.ds(i * MOE_W, SUB)
            kept = pl.ds(i * SUB, SUB)
            add = jnp.where((p == 0) & (off[i] > 0), carry[kept, :].astype(F32), 0.0)
            ybuf[slot, head, :] = (ybuf[slot, head, :].astype(F32) + add).astype(BF16)
            src = pl.multiple_of(i * MOE_W + coff_ref[i * nj + j], SUB)
            carry[kept, :] = jnp.where(p == cpass_ref[i * nj + j], ybuf[slot, pl.ds(src, SUB), :], carry[kept, :])

        @pl.when(k > 0)
        def _():
            wait_batch(1 - slot)

        for i in range(e):
            dst = pl.multiple_of(i * region + s16_ref[i * nj + j] + p * MOE_W, SUB)
            window_copy(slot, i, dst).start()
        nbatch[0] = k + 1
        return carry_in

    lax.fori_loop(0, npass_ref[j], one_pass, 0)


def _ffn_kernel(x_ref, wg_ref, wu_ref, wd_ref, o_ref, acc_ref):
    ei, f = pl.program_id(0), pl.program_id(2)
    d = o_ref.shape[1]

    @pl.when(f == 0)
    def _():
        acc_ref[...] = jnp.zeros_like(acc_ref)

    x = x_ref[:, :d]
    gate = jnp.dot(x, wg_ref[0], preferred_element_type=F32)
    up = jnp.dot(x, wu_ref[0], preferred_element_type=F32)
    hid = (gate * jax.nn.sigmoid(gate) * up).astype(BF16)
    acc_ref[...] += jnp.dot(hid, wd_ref[0], preferred_element_type=F32)

    @pl.when(f == pl.num_programs(2) - 1)
    def _():
        n_e = pl.num_programs(0)
        lane = lax.broadcasted_iota(I32, (1, GATE_COLS), 1)
        own = (lane < 3 * n_e) & (lane % n_e == ei)
        g = jnp.sum(jnp.where(own, x_ref[:, d:].astype(F32), 0.0), axis=1, keepdims=True)
        o_ref[...] = (acc_ref[...] * g).astype(BF16)


def _combine_kernel(a_ref, s16_ref, npass_ref, x_ref, pos_ref, gfin_ref, o_hbm, out_ref, obuf, sem,
                    *, region, cap, nj, final_norm):
    e = pos_ref.shape[0]
    t = MOE_T
    tiles = pos_ref.shape[1] // t
    assert tiles % 2 == 0 or tiles == 1
    step, nsteps = pl.program_id(0), pl.num_programs(0)

    def window_start(i, jj, p):
        return jnp.minimum(s16_ref[i * nj + jj] + p * MOE_W, cap - MOE_W)

    def window_copy(i, src, buf):
        return pltpu.make_async_copy(o_hbm.at[pl.ds(src, MOE_W)], obuf.at[buf, pl.ds(i * MOE_W, MOE_W)], sem.at[buf])

    def fetch(jj, p, buf):
        for i in range(e):
            window_copy(i, pl.multiple_of(i * region + window_start(i, jj, p), SUB), buf).start()

    def wait(buf):
        for i in range(e):
            window_copy(i, 0, buf).wait()

    def one_hot(j, pos_tile, p):
        lo, hi, row0 = [], [], []
        for i in range(e):
            a = a_ref[i * nj + j]
            first = s16_ref[i * nj + j] + p * MOE_W - a
            lo.append(first)
            hi.append(first + MOE_W)
            row0.append(a - window_start(i, j, p))
        return _one_hot_slots(pos_tile, row0, lo, hi)

    @pl.when(step == 0)
    def _():
        fetch(0, 0, 0)

    for u in range(tiles):
        j = step * tiles + u
        slot = (u % 2) if tiles > 1 else step % 2
        rows = pl.ds(u * t, t)
        pos_tile = pos_ref.at[:, rows]
        if u + 1 < tiles:
            fetch(j + 1, 0, 1 - slot)
        else:
            @pl.when(step + 1 < nsteps)
            def _():
                fetch(j + 1, 0, 1 - slot)

        onehot = one_hot(j, pos_tile, 0)
        wait(slot)
        y = x_ref[rows, :] + lax.dot_general(onehot, obuf[slot], (((0,), (0,)), ((), ())),
                                             preferred_element_type=F32)

        def extra_pass(p, acc, j=j, pos_tile=pos_tile):
            fetch(j, p, 2)
            oh = one_hot(j, pos_tile, p)
            wait(2)
            return acc + lax.dot_general(oh, obuf[2], (((0,), (0,)), ((), ())), preferred_element_type=F32)

        y = lax.fori_loop(1, npass_ref[j], extra_pass, y)
        out_ref[rows, :] = _rms(y, gfin_ref[...]) if final_norm else y


def _largest_tile(n, options):
    for t in options:
        if n % t == 0:
            return t
    raise ValueError(f"no tile in {options} divides {n}")


def expert_choice_moe(x, h, probs_t, w_gate, w_up, w_down, layer, g_final=None):
    n, d = x.shape
    e = probs_t.shape[0]
    f = w_gate.shape[3]
    cap = EC_CAPACITY * n // e
    nj = n // MOE_T
    da = d + GATE_COLS
    assert 3 * e <= GATE_COLS and cap % SUB == 0 and cap >= MOE_W
    gates, pos, cnt = expert_choice_select(probs_t)

    ends = jnp.cumsum(cnt, axis=1)
    a = ends - cnt
    off = a % SUB
    s16 = a - off
    tot = off + cnt
    group = tot // SUB * SUB
    cpass = jnp.where(tot % SUB != 0, group // MOE_W, -1)
    coff = group % MOE_W
    npass = jnp.maximum(1, jnp.max((tot + MOE_W - 1) // MOE_W, axis=0))
    flat = lambda v: v.astype(I32).reshape(-1)

    tm = _largest_tile(cap, (1024, 512, 256, 128))
    zr = 256
    region = cap + -(-(MOE_T + 2 * MOE_W) // max(tm, zr)) * max(tm, zr)
    mt = cap // tm

    tps = _largest_tile(nj, (4, 2, 1))
    xs = pl.pallas_call(
        functools.partial(_gather_kernel, region=region, cap=cap, nj=nj),
        grid_spec=pltpu.PrefetchScalarGridSpec(
            num_scalar_prefetch=5,
            grid=(nj // tps,),
            in_specs=[pl.BlockSpec((tps * MOE_T, d), lambda i, *_: (i, 0)),
                      pl.BlockSpec((e, tps * MOE_T), lambda i, *_: (0, i)),
                      pl.BlockSpec((e, tps * MOE_T), lambda i, *_: (0, i))],
            out_specs=pl.BlockSpec(memory_space=pl.ANY),
            scratch_shapes=[pltpu.VMEM((2, e * MOE_W, da), BF16),
                            pltpu.VMEM((e * SUB, da), BF16),
                            pltpu.VMEM((zr, da), BF16),
                            pltpu.SMEM((1,), I32),
                            pltpu.SemaphoreType.DMA((2,))]),
        out_shape=jax.ShapeDtypeStruct((e * region, da), BF16),
        compiler_params=_cparams(("arbitrary",)),
    )(flat(s16), flat(off), flat(cpass), flat(coff), npass.astype(I32), h, gates, pos)

    fc = f // 2 if (f > 1024 and (f // 2) % 128 == 0) else f
    ft = f // fc
    rt = region // tm
    out_slots = pl.pallas_call(
        _ffn_kernel,
        grid=(e, mt, ft),
        in_specs=[pl.BlockSpec((tm, da), lambda ei, m, fi: (ei * rt + m, 0)),
                  pl.BlockSpec((None, 1, d, fc), lambda ei, m, fi: (layer, ei, 0, fi)),
                  pl.BlockSpec((None, 1, d, fc), lambda ei, m, fi: (layer, ei, 0, fi)),
                  pl.BlockSpec((None, 1, fc, d), lambda ei, m, fi: (layer, ei, fi, 0))],
        out_specs=pl.BlockSpec((tm, d), lambda ei, m, fi: (ei * mt + m, 0)),
        out_shape=jax.ShapeDtypeStruct((e * cap, d), BF16),
        scratch_shapes=[pltpu.VMEM((tm, d), F32)],
        compiler_params=_cparams(("parallel", "parallel", "arbitrary")),
    )(xs, w_gate, w_up, w_down)

    final_norm = g_final is not None
    gfin = (g_final if final_norm else jnp.ones((d,), F32)).reshape(1, d)
    return pl.pallas_call(
        functools.partial(_combine_kernel, region=cap, cap=cap, nj=nj, final_norm=final_norm),
        grid_spec=pltpu.PrefetchScalarGridSpec(
            num_scalar_prefetch=3,
            grid=(nj // tps,),
            in_specs=[pl.BlockSpec((tps * MOE_T, d), lambda i, *_: (i, 0)),
                      pl.BlockSpec((e, tps * MOE_T), lambda i, *_: (0, i)),
                      pl.BlockSpec((1, d), lambda i, *_: (0, 0)),
                      pl.BlockSpec(memory_space=pl.ANY)],
            out_specs=pl.BlockSpec((tps * MOE_T, d), lambda i, *_: (i, 0)),
            scratch_shapes=[pltpu.VMEM((3, e * MOE_W, d), BF16),
                            pltpu.SemaphoreType.DMA((3,))]),
        out_shape=jax.ShapeDtypeStruct((n, d), F32),
        compiler_params=_cparams(("arbitrary",)),
    )(flat(a), flat(s16), npass.astype(I32), x, pos, gfin, out_slots)


def _trunk(x3, params):
    batch, seq, d = x3.shape
    x = x3.reshape(batch * seq, d)
    depth = params["norm_mix"].shape[0]
    for i in range(depth):
        jm = i // 2
        if i % 2 == 0:
            a = fourier_core(x, params["norm_mix"][i], params["w_fourier_in"][jm], batch, seq)
            w_o = params["w_fourier_out"][jm]
        else:
            qkv = norm_matmul(x, params["norm_mix"][i], params["w_qkv"][jm])
            a = attention_core(qkv, params["attn_tables"][jm], batch, seq)
            w_o = params["w_attn_out"][jm]
        x, h, probs_t = post_mixer(a, x, w_o, params["norm_ffn"][i], params["w_router2"][i], params["n_experts"])
        g_final = params["norm_final"] if i == depth - 1 else None
        x = expert_choice_moe(x, h, probs_t, params["w_gate"], params["w_up"], params["w_down"], i, g_final)
    return x.reshape(batch, seq, d)


def kernel(x_prompt, x_sample, norm_mix, norm_ffn, norm_final, w_fourier_in, w_fourier_out, w_qkv, w_attn_out,
           rel_pos_bias, w_router, w_gate, w_up, w_down):
    params = dict(
        norm_mix=norm_mix, norm_ffn=norm_ffn, norm_final=norm_final,
        attn_tables=[attention_tables(rel_pos_bias[i]) for i in range(rel_pos_bias.shape[0])],
        w_fourier_in=w_fourier_in.astype(BF16), w_fourier_out=w_fourier_out.astype(BF16),
        w_qkv=w_qkv.astype(BF16), w_attn_out=w_attn_out.astype(BF16),
        w_router2=[pad_router_weights(w_router[i]) for i in range(w_router.shape[0])],
        n_experts=w_router.shape[2],
        w_gate=w_gate.astype(BF16), w_up=w_up.astype(BF16), w_down=w_down.astype(BF16))
    return (_trunk(x_prompt, params), _trunk(x_sample, params))
```

```python
import functools
import math

import numpy as np
import jax
import jax.numpy as jnp
from jax import lax
from jax.experimental import pallas as pl
from jax.experimental.pallas import tpu as pltpu

EPS = 1e-6
GRID_W = 64
N_HEADS = 16
HEAD_DIM = 64
NA_KH = 8
NA_KW = 16
F_GROUPS = 4
EC_CAPACITY = 2
NEG_INF = -1e30

ROW_TILE = 1024
ATT_ROWS = 4
MOE_T = 256
MOE_W = 64
SUB = 16
LANES = 128
GATE_COLS = 128
VMEM_LIMIT = 56 * 1024 * 1024

F32 = jnp.float32
BF16 = jnp.bfloat16
I32 = jnp.int32


def _cparams(sem):
    return pltpu.CompilerParams(dimension_semantics=sem, vmem_limit_bytes=VMEM_LIMIT)


def _rms(x, g):
    r = lax.rsqrt(jnp.mean(x * x, axis=-1, keepdims=True) + EPS)
    return (x * r) * g


def _norm_matmul_kernel(x_ref, g_ref, w_ref, o_ref, *, col_chunk):
    h = _rms(x_ref[...], g_ref[...]).astype(BF16)
    for c in range(o_ref.shape[1] // col_chunk):
        sl = slice(c * col_chunk, (c + 1) * col_chunk)
        o_ref[:, sl] = jnp.dot(h, w_ref[:, sl], preferred_element_type=F32).astype(o_ref.dtype)


def norm_matmul(x, g, w):
    n, d = x.shape
    m = w.shape[1]
    return pl.pallas_call(
        functools.partial(_norm_matmul_kernel, col_chunk=min(m, 1024)),
        grid=(n // ROW_TILE,),
        in_specs=[pl.BlockSpec((ROW_TILE, d), lambda i: (i, 0)),
                  pl.BlockSpec((1, d), lambda i: (0, 0)),
                  pl.BlockSpec((d, m), lambda i: (0, 0))],
        out_specs=pl.BlockSpec((ROW_TILE, m), lambda i: (i, 0)),
        out_shape=jax.ShapeDtypeStruct((n, m), BF16),
        compiler_params=_cparams(("parallel",)),
    )(x, g.reshape(1, d), w)


def _fourier_in_kernel(x_ref, g_ref, w_ref, cs_ref, vr_ref, vi_ref):
    h = _rms(x_ref[...], g_ref[...]).astype(BF16)
    u = jnp.dot(h, w_ref[...], preferred_element_type=F32).astype(BF16)
    gd = cs_ref.shape[0]
    for gi in range(u.shape[1] // gd):
        sl = slice(gi * gd, (gi + 1) * gd)
        ab = jnp.dot(u[:, sl], cs_ref[...], preferred_element_type=F32)
        vr_ref[:, sl] = ab[:, :gd].astype(BF16)
        vi_ref[:, sl] = ab[:, gd:].astype(BF16)


def _dft_a_kernel(vr_ref, vi_ref, m_ref, yr_ref, yi_ref, in_r, in_i, out_r, out_i):
    n1, t2, dc = vr_ref.shape
    chunks = [slice(c * LANES, (c + 1) * LANES) for c in range(dc // LANES)]
    for c, sl in enumerate(chunks):
        in_r[c] = vr_ref[:, :, sl].astype(F32).reshape(n1 * t2, LANES)
        in_i[c] = vi_ref[:, :, sl].astype(F32).reshape(n1 * t2, LANES)
    for s in range(t2):
        rows = pl.ds(s, n1, stride=t2)
        v = jnp.concatenate([jnp.concatenate([ref[c, rows, :] for c in range(len(chunks))], axis=1)
                             for ref in (in_r, in_i)], axis=0).astype(BF16)
        y = jnp.dot(m_ref[...], v, preferred_element_type=F32)
        for c, sl in enumerate(chunks):
            out_r[c, rows, :] = y[:n1, sl]
            out_i[c, rows, :] = y[n1:, sl]
    for c, sl in enumerate(chunks):
        yr_ref[:, :, sl] = out_r[c].reshape(n1, t2, LANES).astype(BF16)
        yi_ref[:, :, sl] = out_i[c].reshape(n1, t2, LANES).astype(BF16)


def _dft_c_kernel(yr_ref, yi_ref, g_ref, z_ref, out_z):
    kb, n2, d = yr_ref.shape
    chunks = [slice(c * LANES, (c + 1) * LANES) for c in range(d // LANES)]
    for i in range(kb):
        ycat = jnp.concatenate([yr_ref[i], yi_ref[i]], axis=0)
        z = jnp.dot(g_ref[i], ycat, preferred_element_type=F32)
        for c, sl in enumerate(chunks):
            out_z[c, pl.ds(i, n2, stride=kb), :] = z[:, sl]
    for c, sl in enumerate(chunks):
        z_ref[:, :, sl] = out_z[c].reshape(n2, kb, LANES).astype(BF16)


def _dft_tables(s, n1, n2, gd):
    def cs(num, den):
        ang = (2.0 * math.pi / den) * (num % den).astype(F32)
        return jnp.cos(ang), jnp.sin(ang)

    ic = jnp.arange(gd, dtype=I32)
    cc, sc = cs(ic[:, None] * ic[None, :], gd)
    chan = jnp.concatenate([cc, -sc], axis=1).astype(BF16)
    i1 = jnp.arange(n1, dtype=I32)
    c1, s1 = cs(i1[:, None] * i1[None, :], n1)
    m1 = jnp.concatenate([jnp.concatenate([c1, s1], axis=1),
                          jnp.concatenate([-s1, c1], axis=1)], axis=0).astype(BF16)
    k1 = jnp.arange(n1, dtype=I32)[:, None, None]
    k2 = jnp.arange(n2, dtype=I32)[None, :, None]
    s2 = jnp.arange(n2, dtype=I32)[None, None, :]
    cg, sg = cs(((n1 * k2 + k1) % s) * s2, s)
    scale = 1.0 / math.sqrt(float(s) * float(gd))
    g = (jnp.concatenate([cg, sg], axis=2) * scale).astype(BF16)
    return chan, m1, g


def fourier_core(x, g_norm, w_in, batch, seq):
    n, d = x.shape
    gd = d // F_GROUPS
    n1 = 1 << ((seq.bit_length() - 1 + 1) // 2)
    n2 = seq // n1
    assert n1 * n2 == seq and n1 % 16 == 0 and n2 % 16 == 0
    chan, m1, gtab = _dft_tables(seq, n1, n2, gd)

    vr, vi = pl.pallas_call(
        _fourier_in_kernel,
        grid=(n // ROW_TILE,),
        in_specs=[pl.BlockSpec((ROW_TILE, d), lambda i: (i, 0)),
                  pl.BlockSpec((1, d), lambda i: (0, 0)),
                  pl.BlockSpec((d, d), lambda i: (0, 0)),
                  pl.BlockSpec((gd, 2 * gd), lambda i: (0, 0))],
        out_specs=[pl.BlockSpec((ROW_TILE, d), lambda i: (i, 0))] * 2,
        out_shape=[jax.ShapeDtypeStruct((n, d), BF16)] * 2,
        compiler_params=_cparams(("parallel",)),
    )(x, g_norm.reshape(1, d), w_in, chan)

    t2, dc = SUB, min(d, 512)
    a_spec = pl.BlockSpec((n1, t2, dc), lambda b, j, c: (b, j, c))
    yr, yi = pl.pallas_call(
        _dft_a_kernel,
        grid=(batch, n2 // t2, d // dc),
        in_specs=[a_spec, a_spec, pl.BlockSpec((2 * n1, 2 * n1), lambda b, j, c: (0, 0))],
        out_specs=[a_spec, a_spec],
        out_shape=[jax.ShapeDtypeStruct((batch * n1, n2, d), BF16)] * 2,
        scratch_shapes=[pltpu.VMEM((dc // LANES, n1 * t2, LANES), F32) for _ in range(4)],
        compiler_params=_cparams(("parallel", "parallel", "parallel")),
    )(vr.reshape(batch * n1, n2, d), vi.reshape(batch * n1, n2, d), m1)

    kb = SUB
    c_spec = pl.BlockSpec((kb, n2, d), lambda b, k: (b * (n1 // kb) + k, 0, 0))
    z = pl.pallas_call(
        _dft_c_kernel,
        grid=(batch, n1 // kb),
        in_specs=[c_spec, c_spec, pl.BlockSpec((kb, n2, 2 * n2), lambda b, k: (k, 0, 0))],
        out_specs=pl.BlockSpec((n2, kb, d), lambda b, k: (b, k, 0)),
        out_shape=jax.ShapeDtypeStruct((batch * n2, n1, d), BF16),
        scratch_shapes=[pltpu.VMEM((d // LANES, n2 * kb, LANES), F32)],
        compiler_params=_cparams(("parallel", "parallel")),
    )(yr, yi, gtab)
    return z.reshape(n, d)


def _attn_kernel(q_ref, kp_ref, kc_ref, kn_ref, vp_ref, vc_ref, vn_ref, bias_ref, o_ref):
    rq = q_ref.shape[0]
    q = q_ref[...] * jnp.asarray(HEAD_DIM ** -0.5, BF16)
    k = jnp.concatenate([kp_ref[...], kc_ref[...], kn_ref[...]], axis=0)
    v = jnp.concatenate([vp_ref[...], vc_ref[...], vn_ref[...]], axis=0)
    first = lax.broadcasted_iota(I32, (1, 2 * HEAD_DIM), 1) < HEAD_DIM
    for hp in range(N_HEADS // 2):
        sl = slice(hp * 2 * HEAD_DIM, (hp + 1) * 2 * HEAD_DIM)
        q2, k2, v2 = q[:, sl], k[:, sl], v[:, sl]
        zero = jnp.zeros_like(q2)
        qs = jnp.concatenate([jnp.where(first, q2, zero), jnp.where(first, zero, q2)], axis=0)
        sc = lax.dot_general(qs, k2, (((1,), (1,)), ((), ())), preferred_element_type=F32)
        sc = sc + bias_ref[0, hp]
        m = jnp.max(sc, axis=1, keepdims=True)
        p = jnp.exp(sc - m)
        l = jnp.sum(p, axis=1, keepdims=True)
        pv = jnp.dot(p.astype(BF16), v2, preferred_element_type=F32) / l
        o_ref[:, sl] = jnp.where(first, pv[:rq], pv[rq:]).astype(BF16)


def attention_tables(rpb):
    h = rpb.shape[0]
    wk = 3 * ATT_ROWS
    idx = np.clip(np.arange(2 * GRID_W - 1) - (GRID_W - NA_KW), 0, 2 * NA_KW - 2)
    v = jnp.concatenate([rpb[:, :, idx], jnp.zeros(rpb.shape[:2] + (1,), rpb.dtype)], axis=-1)
    m = jnp.tile(v, (1, 1, GRID_W))[:, :, :GRID_W * (2 * GRID_W - 1)]
    blk = m.reshape(h, 2 * NA_KH - 1, GRID_W, 2 * GRID_W - 1)[:, :, :, GRID_W - 1:]
    lo = NA_KH - 1 - ATT_ROWS
    per_row = jnp.stack([blk[:, lo - iq: lo - iq + wk] for iq in range(ATT_ROWS)], axis=1)
    bias = jnp.transpose(per_row, (0, 1, 3, 2, 4)).reshape(h, ATT_ROWS * GRID_W, wk * GRID_W).astype(F32)

    bias = bias.reshape(h // 2, 2 * ATT_ROWS * GRID_W, wk * GRID_W)

    iq = np.arange(ATT_ROWS)[:, None, None, None]
    c = np.arange(GRID_W)[None, :, None, None]
    kr = np.arange(wk)[None, None, :, None]
    kc = np.arange(GRID_W)[None, None, None, :]
    shape = (ATT_ROWS, GRID_W, wk, GRID_W)
    ws = np.clip(c - NA_KW // 2, 0, GRID_W - NA_KW)
    col_ok = (kc >= ws) & (kc < ws + NA_KW)
    row_ok = [
        (kr >= ATT_ROWS) & (kr < ATT_ROWS + NA_KH) & (iq >= 0),
        (kr >= iq) & (kr < iq + NA_KH),
        (kr < NA_KH) & (iq >= 0),
    ]
    mask = np.stack([np.where(np.broadcast_to(r & col_ok, shape), 0.0, NEG_INF).reshape(ATT_ROWS * GRID_W, -1)
                     for r in row_ok]).astype(np.float32)
    return bias[None] + jnp.asarray(np.tile(mask, (1, 2, 1)))[:, None]


def attention_core(qkv, tables, batch, seq):
    n = qkv.shape[0]
    d = N_HEADS * HEAD_DIM
    rows = seq // GRID_W
    groups = rows // ATT_ROWS
    assert rows % ATT_ROWS == 0 and rows >= 4 * ATT_ROWS and NA_KH == 2 * ATT_ROWS
    rq = ATT_ROWS * GRID_W
    bias = tables

    def blk(off, col):
        def index_map(b, g):
            return (b * groups + jnp.clip(g + off, 0, groups - 1), col)
        return pl.BlockSpec((rq, d), index_map)

    def case_map(b, g):
        return (jnp.where(g == 0, 0, jnp.where(g == groups - 1, 2, 1)), 0, 0, 0)

    return pl.pallas_call(
        _attn_kernel,
        grid=(batch, groups),
        in_specs=[blk(0, 0), blk(-1, 1), blk(0, 1), blk(1, 1), blk(-1, 2), blk(0, 2), blk(1, 2),
                  pl.BlockSpec((1, N_HEADS // 2, 2 * rq, 3 * rq), case_map)],
        out_specs=pl.BlockSpec((rq, d), lambda b, g: (b * groups + g, 0)),
        out_shape=jax.ShapeDtypeStruct((n, d), BF16),
        compiler_params=_cparams(("parallel", "parallel")),
    )(qkv, qkv, qkv, qkv, qkv, qkv, qkv, bias)


def _post_kernel(a_ref, x_ref, wo_ref, g_ref, wr_ref, xo_ref, h_ref, p_ref):
    xo_ref[...] = x_ref[...] + jnp.dot(a_ref[...], wo_ref[...], preferred_element_type=F32)
    for r in range(x_ref.shape[0] // LANES):
        rows = slice(r * LANES, (r + 1) * LANES)
        hf = _rms(xo_ref[rows, :], g_ref[...])
        h_hi = hf.astype(BF16)
        h_ref[rows, :] = h_hi
        lg = jnp.dot(h_hi, wr_ref[...], preferred_element_type=F32)
        logits = lg.T[:p_ref.shape[0]]
        m = jnp.max(logits, axis=0, keepdims=True)
        e = jnp.exp(logits - m)
        p_ref[:, rows] = e / jnp.sum(e, axis=0, keepdims=True)


def pad_router_weights(w_router):
    return jnp.pad(w_router.astype(BF16), ((0, 0), (0, LANES - w_router.shape[1])))


def post_mixer(a, x, w_o, g_ffn, w_router2, e):
    n, d = x.shape
    return pl.pallas_call(
        _post_kernel,
        grid=(n // ROW_TILE,),
        in_specs=[pl.BlockSpec((ROW_TILE, d), lambda i: (i, 0)),
                  pl.BlockSpec((ROW_TILE, d), lambda i: (i, 0)),
                  pl.BlockSpec((d, d), lambda i: (0, 0)),
                  pl.BlockSpec((1, d), lambda i: (0, 0)),
                  pl.BlockSpec((d, LANES), lambda i: (0, 0))],
        out_specs=[pl.BlockSpec((ROW_TILE, d), lambda i: (i, 0)),
                   pl.BlockSpec((ROW_TILE, d), lambda i: (i, 0)),
                   pl.BlockSpec((e, ROW_TILE), lambda i: (0, i))],
        out_shape=[jax.ShapeDtypeStruct((n, d), F32),
                   jax.ShapeDtypeStruct((n, d), BF16),
                   jax.ShapeDtypeStruct((e, n), F32)],
        compiler_params=_cparams(("parallel",)),
    )(a, x, w_o, g_ffn.reshape(1, d), w_router2)


def _threshold_kernel(p_ref, thr_ref, need_ref, *, cap):
    e = p_ref.shape[0]

    def count_ge(cand):
        bits = pltpu.bitcast(p_ref[...], I32)
        return jnp.sum((bits >= cand).astype(F32), axis=1, keepdims=True)

    def body(i, prefix):
        cand = prefix | jnp.left_shift(jnp.int32(1), 30 - i)
        return jnp.where(count_ge(cand) >= cap, cand, prefix)

    thr = lax.fori_loop(0, 31, body, jnp.zeros((e, 1), I32))
    n_gt = count_ge(thr + 1)
    thr_ref[...] = jnp.broadcast_to(thr, thr_ref.shape)
    need_ref[...] = jnp.broadcast_to(cap - n_gt, need_ref.shape)


def _select_kernel(p_ref, thr_ref, need_ref, ut_ref, gate_ref, pos_ref, cnt_ref, carry_ref):
    @pl.when(pl.program_id(0) == 0)
    def _():
        carry_ref[...] = jnp.zeros_like(carry_ref)

    thr = thr_ref[:, :1]
    ut = ut_ref[...]
    t = ut.shape[0]
    for i in range(cnt_ref.shape[0]):
        cols = slice(i * t, (i + 1) * t)
        p = p_ref[:, cols]
        bits = pltpu.bitcast(p, I32)
        gt = bits > thr
        eq = bits == thr
        eq_f = jnp.where(eq, 1.0, 0.0)
        eq_before = jnp.dot(eq_f.astype(BF16), ut, preferred_element_type=F32) + carry_ref[:, :1]
        sel = gt | (eq & (eq_before < need_ref[:, :1]))
        carry_ref[...] = carry_ref[...] + jnp.sum(eq_f, axis=1, keepdims=True)
        sel_f = jnp.where(sel, 1.0, 0.0)
        rank = jnp.dot(sel_f.astype(BF16), ut, preferred_element_type=F32)
        pos_ref[:, cols] = jnp.where(sel, rank.astype(I32), -1)
        gate_ref[:, cols] = jnp.where(sel, p, 0.0)
        cnt = jnp.sum(sel_f, axis=1, keepdims=True).astype(I32)
        cnt_ref[i] = jnp.broadcast_to(cnt, cnt_ref.shape[1:])


def expert_choice_select(probs_t):
    e, n = probs_t.shape
    cap = EC_CAPACITY * n // e
    j = n // MOE_T
    thr, need = pl.pallas_call(
        functools.partial(_threshold_kernel, cap=cap),
        out_shape=[jax.ShapeDtypeStruct((e, 128), I32), jax.ShapeDtypeStruct((e, 128), F32)],
        compiler_params=pltpu.CompilerParams(vmem_limit_bytes=VMEM_LIMIT),
    )(probs_t)
    it = np.arange(MOE_T)
    ut = jnp.asarray((it[:, None] < it[None, :]).astype(np.float32), BF16)
    tiles = _largest_tile(j, (8, 4, 2, 1))
    gates, pos, cnt = pl.pallas_call(
        _select_kernel,
        grid=(j // tiles,),
        in_specs=[pl.BlockSpec((e, tiles * MOE_T), lambda i: (0, i)),
                  pl.BlockSpec((e, 128), lambda i: (0, 0)),
                  pl.BlockSpec((e, 128), lambda i: (0, 0)),
                  pl.BlockSpec((MOE_T, MOE_T), lambda i: (0, 0))],
        out_specs=[pl.BlockSpec((e, tiles * MOE_T), lambda i: (0, i)),
                   pl.BlockSpec((e, tiles * MOE_T), lambda i: (0, i)),
                   pl.BlockSpec((tiles, e, 128), lambda i: (i, 0, 0))],
        out_shape=[jax.ShapeDtypeStruct((e, n), F32),
                   jax.ShapeDtypeStruct((e, n), I32),
                   jax.ShapeDtypeStruct((j, e, 128), I32)],
        scratch_shapes=[pltpu.VMEM((e, 128), F32)],
        compiler_params=_cparams(("arbitrary",)),
    )(probs_t, thr, need, ut)
    return gates, pos, cnt[:, :, 0].T


def _one_hot_slots(pos_ref, row_of_rank0, lo, hi):
    e, t = pos_ref.shape
    rows = lax.broadcasted_iota(I32, (MOE_W, t), 0)
    blocks = []
    for i in range(e):
        r = pos_ref[i:i + 1, :]
        tgt = jnp.where((r >= jnp.maximum(lo[i], 0)) & (r < hi[i]), r + row_of_rank0[i], -1)
        blocks.append(jnp.where(rows == tgt, 1.0, 0.0).astype(BF16))
    return jnp.concatenate(blocks, axis=0)


def _gather_kernel(s16_ref, off_ref, cpass_ref, coff_ref, npass_ref, h_ref, gate_ref, pos_ref, xs_ref,
                   ybuf, carry, zbuf, nbatch, sem, *, region, cap, nj):
    e = pos_ref.shape[0]
    t = MOE_T
    d = h_ref.shape[1]
    tiles = pos_ref.shape[1] // t

    @pl.when(pl.program_id(0) == 0)
    def _():
        nbatch[0] = 0
        carry[...] = jnp.zeros_like(carry)

    def window_copy(slot, i, dst):
        return pltpu.make_async_copy(ybuf.at[slot, pl.ds(i * MOE_W, MOE_W)], xs_ref.at[pl.ds(dst, MOE_W)], sem.at[0])

    def wait_batch(slot):
        for i in range(e):
            window_copy(slot, i, 0).wait()

    for u in range(tiles):
        _gather_tile(pl.program_id(0) * tiles + u, nj, region,
                     s16_ref, off_ref, cpass_ref, coff_ref, npass_ref,
                     h_ref.at[pl.ds(u * t, t), :], gate_ref[:, u * t:(u + 1) * t], pos_ref.at[:, pl.ds(u * t, t)],
                     ybuf, carry, nbatch, window_copy, wait_batch)

    @pl.when(pl.program_id(0) == pl.num_programs(0) - 1)
    def _():
        wait_batch((nbatch[0] - 1) % 2)
        zbuf[...] = jnp.zeros_like(zbuf)
        zr = zbuf.shape[0]
        cps = [pltpu.make_async_copy(zbuf, xs_ref.at[pl.ds(i * region + cap + c * zr, zr)], sem.at[1])
               for i in range(e) for c in range((region - cap) // zr)]
        for cp in cps:
            cp.start()
        for cp in cps:
            cp.wait()


def _gather_tile(j, nj, region, s16_ref, off_ref, cpass_ref, coff_ref, npass_ref, h_ref, g, pos_ref,
                 ybuf, carry, nbatch, window_copy, wait_batch):
    e, t = pos_ref.shape
    d = h_ref.shape[1]
    g_hi = g.astype(BF16)
    r1 = g - g_hi.astype(F32)
    g_mid = r1.astype(BF16)
    g_lo = (r1 - g_mid.astype(F32)).astype(BF16)
    g3 = jnp.concatenate([g_hi, g_mid, g_lo, jnp.zeros((GATE_COLS - 3 * e, t), BF16)], axis=0)

    def one_pass(p, carry_in):
        k = nbatch[0]
        slot = k % 2
        off = [off_ref[i * nj + j] for i in range(e)]
        shift = [off[i] - p * MOE_W for i in range(e)]
        onehot = _one_hot_slots(pos_ref, shift, [-shift[i] for i in range(e)], [MOE_W - shift[i] for i in range(e)])
        ybuf[slot, :, :d] = jnp.dot(onehot, h_ref[...], preferred_element_type=F32).astype(BF16)
        ybuf[slot, :, d:] = lax.dot_general(onehot, g3, (((1,), (1,)), ((), ())),
                                            preferred_element_type=F32).astype(BF16)
        for i in range(e):
            head = pl.ds(i * MOE_W, SUB)
            kept = pl.ds(i * SUB, SUB)
            add = jnp.where((p == 0) & (off[i] > 0), carry[kept, :].astype(F32), 0.0)
            ybuf[slot, head, :] = (ybuf[slot, head, :].astype(F32) + add).astype(BF16)
            src = pl.multiple_of(i * MOE_W + coff_ref[i * nj + j], SUB)
            carry[kept, :] = jnp.where(p == cpass_ref[i * nj + j], ybuf[slot, pl.ds(src, SUB), :], carry[kept, :])

        @pl.when(k > 0)
        def _():
            wait_batch(1 - slot)

        for i in range(e):
            dst = pl.multiple_of(i * region + s16_ref[i * nj + j] + p * MOE_W, SUB)
            window_copy(slot, i, dst).start()
        nbatch[0] = k + 1
        return carry_in

    lax.fori_loop(0, npass_ref[j], one_pass, 0)


def _ffn_kernel(x_ref, wg_ref, wu_ref, wd_ref, o_ref, acc_ref):
    ei, f = pl.program_id(0), pl.program_id(2)
    d = o_ref.shape[1]

    @pl.when(f == 0)
    def _():
        acc_ref[...] = jnp.zeros_like(acc_ref)

    x = x_ref[:, :d]
    gate = jnp.dot(x, wg_ref[0], preferred_element_type=F32)
    up = jnp.dot(x, wu_ref[0], preferred_element_type=F32)
    hid = (gate * jax.nn.sigmoid(gate) * up).astype(BF16)
    acc_ref[...] += jnp.dot(hid, wd_ref[0], preferred_element_type=F32)

    @pl.when(f == pl.num_programs(2) - 1)
    def _():
        n_e = pl.num_programs(0)
        lane = lax.broadcasted_iota(I32, (1, GATE_COLS), 1)
        own = (lane < 3 * n_e) & (lane % n_e == ei)
        g = jnp.sum(jnp.where(own, x_ref[:, d:].astype(F32), 0.0), axis=1, keepdims=True)
        o_ref[...] = (acc_ref[...] * g).astype(BF16)


def _combine_kernel(a_ref, s16_ref, npass_ref, x_ref, pos_ref, gfin_ref, o_hbm, out_ref, obuf, sem,
                    *, region, cap, final_norm):
    j = pl.program_id(0)
    nj = pl.num_programs(0)
    e = pos_ref.shape[0]
    slot = j % 2

    def window_start(i, jj, p):
        return jnp.minimum(s16_ref[i * nj + jj] + p * MOE_W, cap - MOE_W)

    def window_copy(i, src, buf):
        return pltpu.make_async_copy(o_hbm.at[pl.ds(src, MOE_W)], obuf.at[buf, pl.ds(i * MOE_W, MOE_W)], sem.at[buf])

    def fetch(jj, p, buf):
        for i in range(e):
            window_copy(i, pl.multiple_of(i * region + window_start(i, jj, p), SUB), buf).start()

    def wait(buf):
        for i in range(e):
            window_copy(i, 0, buf).wait()

    def one_hot(p):
        lo, hi, row0 = [], [], []
        for i in range(e):
            a = a_ref[i * nj + j]
            first = s16_ref[i * nj + j] + p * MOE_W - a
            lo.append(first)
            hi.append(first + MOE_W)
            row0.append(a - window_start(i, j, p))
        return _one_hot_slots(pos_ref, row0, lo, hi)

    @pl.when(j == 0)
    def _():
        fetch(0, 0, 0)

    @pl.when(j + 1 < nj)
    def _():
        fetch(j + 1, 0, 1 - slot)

    onehot = one_hot(0)
    wait(slot)
    y = x_ref[...] + lax.dot_general(onehot, obuf[slot], (((0,), (0,)), ((), ())), preferred_element_type=F32)

    def extra_pass(p, acc):
        fetch(j, p, 2)
        oh = one_hot(p)
        wait(2)
        return acc + lax.dot_general(oh, obuf[2], (((0,), (0,)), ((), ())), preferred_element_type=F32)

    y = lax.fori_loop(1, npass_ref[j], extra_pass, y)
    out_ref[...] = _rms(y, gfin_ref[...]) if final_norm else y


def _largest_tile(n, options):
    for t in options:
        if n % t == 0:
            return t
    raise ValueError(f"no tile in {options} divides {n}")


def expert_choice_moe(x, h, probs_t, w_gate, w_up, w_down, layer, g_final=None):
    n, d = x.shape
    e = probs_t.shape[0]
    f = w_gate.shape[3]
    cap = EC_CAPACITY * n // e
    nj = n // MOE_T
    da = d + GATE_COLS
    assert 3 * e <= GATE_COLS and cap % SUB == 0 and cap >= MOE_W
    gates, pos, cnt = expert_choice_select(probs_t)

    ends = jnp.cumsum(cnt, axis=1)
    a = ends - cnt
    off = a % SUB
    s16 = a - off
    tot = off + cnt
    group = tot // SUB * SUB
    cpass = jnp.where(tot % SUB != 0, group // MOE_W, -1)
    coff = group % MOE_W
    npass = jnp.maximum(1, jnp.max((tot + MOE_W - 1) // MOE_W, axis=0))
    flat = lambda v: v.astype(I32).reshape(-1)

    tm = _largest_tile(cap, (1024, 512, 256, 128))
    zr = 256
    region = cap + -(-(MOE_T + 2 * MOE_W) // max(tm, zr)) * max(tm, zr)
    mt = cap // tm

    tps = _largest_tile(nj, (4, 2, 1))
    xs = pl.pallas_call(
        functools.partial(_gather_kernel, region=region, cap=cap, nj=nj),
        grid_spec=pltpu.PrefetchScalarGridSpec(
            num_scalar_prefetch=5,
            grid=(nj // tps,),
            in_specs=[pl.BlockSpec((tps * MOE_T, d), lambda i, *_: (i, 0)),
                      pl.BlockSpec((e, tps * MOE_T), lambda i, *_: (0, i)),
                      pl.BlockSpec((e, tps * MOE_T), lambda i, *_: (0, i))],
            out_specs=pl.BlockSpec(memory_space=pl.ANY),
            scratch_shapes=[pltpu.VMEM((2, e * MOE_W, da), BF16),
                            pltpu.VMEM((e * SUB, da), BF16),
                            pltpu.VMEM((zr, da), BF16),
                            pltpu.SMEM((1,), I32),
                            pltpu.SemaphoreType.DMA((2,))]),
        out_shape=jax.ShapeDtypeStruct((e * region, da), BF16),
        compiler_params=_cparams(("arbitrary",)),
    )(flat(s16), flat(off), flat(cpass), flat(coff), npass.astype(I32), h, gates, pos)

    fc = f // 2 if (f > 1024 and (f // 2) % 128 == 0) else f
    ft = f // fc
    rt = region // tm
    out_slots = pl.pallas_call(
        _ffn_kernel,
        grid=(e, mt, ft),
        in_specs=[pl.BlockSpec((tm, da), lambda ei, m, fi: (ei * rt + m, 0)),
                  pl.BlockSpec((None, 1, d, fc), lambda ei, m, fi: (layer, ei, 0, fi)),
                  pl.BlockSpec((None, 1, d, fc), lambda ei, m, fi: (layer, ei, 0, fi)),
                  pl.BlockSpec((None, 1, fc, d), lambda ei, m, fi: (layer, ei, fi, 0))],
        out_specs=pl.BlockSpec((tm, d), lambda ei, m, fi: (ei * mt + m, 0)),
        out_shape=jax.ShapeDtypeStruct((e * cap, d), BF16),
        scratch_shapes=[pltpu.VMEM((tm, d), F32)],
        compiler_params=_cparams(("parallel", "parallel", "arbitrary")),
    )(xs, w_gate, w_up, w_down)

    final_norm = g_final is not None
    gfin = (g_final if final_norm else jnp.ones((d,), F32)).reshape(1, d)
    return pl.pallas_call(
        functools.partial(_combine_kernel, region=cap, cap=cap, final_norm=final_norm),
        grid_spec=pltpu.PrefetchScalarGridSpec(
            num_scalar_prefetch=3,
            grid=(nj,),
            in_specs=[pl.BlockSpec((MOE_T, d), lambda i, *_: (i, 0)),
                      pl.BlockSpec((e, MOE_T), lambda i, *_: (0, i)),
                      pl.BlockSpec((1, d), lambda i, *_: (0, 0)),
                      pl.BlockSpec(memory_space=pl.ANY)],
            out_specs=pl.BlockSpec((MOE_T, d), lambda i, *_: (i, 0)),
            scratch_shapes=[pltpu.VMEM((3, e * MOE_W, d), BF16),
                            pltpu.SemaphoreType.DMA((3,))]),
        out_shape=jax.ShapeDtypeStruct((n, d), F32),
        compiler_params=_cparams(("arbitrary",)),
    )(flat(a), flat(s16), npass.astype(I32), x, pos, gfin, out_slots)


def _trunk(x3, params):
    batch, seq, d = x3.shape
    x = x3.reshape(batch * seq, d)
    depth = params["norm_mix"].shape[0]
    for i in range(depth):
        jm = i // 2
        if i % 2 == 0:
            a = fourier_core(x, params["norm_mix"][i], params["w_fourier_in"][jm], batch, seq)
            w_o = params["w_fourier_out"][jm]
        else:
            qkv = norm_matmul(x, params["norm_mix"][i], params["w_qkv"][jm])
            a = attention_core(qkv, params["attn_tables"][jm], batch, seq)
            w_o = params["w_attn_out"][jm]
        x, h, probs_t = post_mixer(a, x, w_o, params["norm_ffn"][i], params["w_router2"][i], params["n_experts"])
        g_final = params["norm_final"] if i == depth - 1 else None
        x = expert_choice_moe(x, h, probs_t, params["w_gate"], params["w_up"], params["w_down"], i, g_final)
    return x.reshape(batch, seq, d)


def kernel(x_prompt, x_sample, norm_mix, norm_ffn, norm_final, w_fourier_in, w_fourier_out, w_qkv, w_attn_out,
           rel_pos_bias, w_router, w_gate, w_up, w_down):
    params = dict(
        norm_mix=norm_mix, norm_ffn=norm_ffn, norm_final=norm_final,
        attn_tables=[attention_tables(rel_pos_bias[i]) for i in range(rel_pos_bias.shape[0])],
        w_fourier_in=w_fourier_in.astype(BF16), w_fourier_out=w_fourier_out.astype(BF16),
        w_qkv=w_qkv.astype(BF16), w_attn_out=w_attn_out.astype(BF16),
        w_router2=[pad_router_weights(w_router[i]) for i in range(w_router.shape[0])],
        n_experts=w_router.shape[2],
        w_gate=w_gate.astype(BF16), w_up=w_up.astype(BF16), w_down=w_down.astype(BF16))
    return (_trunk(x_prompt, params), _trunk(x_sample, params))
```
